```python
import jax, jax.numpy as jnp
from jax import lax
import numpy as np

D_MODEL = 1024
BATCH = 16
SEQ = 2048
DEPTH = 4

HEAD_DIM = 64
ATT_WIDTH = D_MODEL // 2
CONV_WIDTH = D_MODEL - ATT_WIDTH
N_ATT_HEADS = ATT_WIDTH // HEAD_DIM
N_KV_GROUPS = 2
HEADS_PER_GROUP = N_ATT_HEADS // N_KV_GROUPS
CONV_K = 3
CMP_BLOCK = 32
CMP_STRIDE = 16
SEL_BLOCK = 64
N_SEL = 8
WINDOW = 512
Q_BLOCK = 128
D_FF = 2816
N_SUB = 3
EPS = 1e-6
NEG_INF = -1e30
FORCE_BONUS = 1e4
Q_COLS = N_ATT_HEADS * HEAD_DIM
KV_COLS = N_KV_GROUPS * HEAD_DIM
GATE_COLS = 3 * N_ATT_HEADS
IN_COLS = Q_COLS + 6 * KV_COLS + GATE_COLS + 3 * CONV_WIDTH

kernel_name = 'hymba_nsa_shortconv_macaron_adaln'


def rms_norm(x, g):
    xf = x.astype(jnp.float32)
    y = xf * lax.rsqrt(jnp.mean(xf * xf, axis=-1, keepdims=True) + EPS)
    return (y * g.astype(jnp.float32)).astype(x.dtype)


def pre_norm(x, g, shift, scale):
    return rms_norm(x, g) * (1 + scale[:, None, :]) + shift[:, None, :]


def swiglu(h, w_in, w_out):
    gate, up = jnp.split(h @ w_in, 2, axis=-1)
    return (jax.nn.silu(gate) * up) @ w_out


def masked_softmax(s, mask):
    s = jnp.where(mask, s.astype(jnp.float32), NEG_INF)
    return jax.nn.softmax(s, axis=-1)


def compress_blocks(k, pos, w1, w2):
    b, s, g, dh = k.shape
    nc = (s - CMP_BLOCK) // CMP_STRIDE + 1
    idx = jnp.arange(nc)[:, None] * CMP_STRIDE + jnp.arange(CMP_BLOCK)[None, :]
    blk = k[:, idx] + pos[None, None, :, None, :]
    flat = blk.transpose(0, 1, 3, 2, 4).reshape(b, nc, g, CMP_BLOCK * dh)
    return jax.nn.silu(flat @ w1) @ w2


def nsa_attention(q, kc, vc, ks, vs, kw, vw, gates, cmp_pos, w_cmp1, w_cmp2):
    b, s = q.shape[:2]
    q = q.reshape(b, s, N_KV_GROUPS, HEADS_PER_GROUP, HEAD_DIM) * (HEAD_DIM ** -0.5)
    t = jnp.arange(s)

    kcmp = compress_blocks(kc, cmp_pos[0], w_cmp1[0], w_cmp2[0])
    vcmp = compress_blocks(vc, cmp_pos[1], w_cmp1[1], w_cmp2[1])
    nc = kcmp.shape[1]
    blk_end = jnp.arange(nc) * CMP_STRIDE + CMP_BLOCK - 1
    cmp_mask = blk_end[None, :] <= t[:, None]
    p_cmp = masked_softmax(jnp.einsum('bsghd,bngd->bghsn', q, kcmp), cmp_mask)
    p_cmp = jnp.where(cmp_mask, p_cmp, 0.0)
    o_cmp = jnp.einsum('bghsn,bngd->bsghd', p_cmp.astype(vcmp.dtype), vcmp)

    nb = s // SEL_BLOCK
    cs = jnp.arange(nc) * CMP_STRIDE
    js = jnp.arange(nb) * SEL_BLOCK
    overlap = ((cs[:, None] < js[None, :] + SEL_BLOCK) &
               (cs[:, None] + CMP_BLOCK > js[None, :])).astype(jnp.float32)
    imp = jnp.einsum('bghsn,nj->bgsj', p_cmp, overlap)
    cur = t // SEL_BLOCK
    jb = jnp.arange(nb)
    blk_ok = jb[None, :] <= cur[:, None]
    forced = ((jb[None, :] == 0) | (jb[None, :] == cur[:, None]) |
              (jb[None, :] == cur[:, None] - 1)).astype(jnp.float32)
    imp = jnp.where(blk_ok, imp + FORCE_BONUS * forced, NEG_INF)
    n_sel = min(N_SEL, nb)
    top_val, top_idx = lax.top_k(imp, n_sel)
    top_ok = top_val > 0.5 * NEG_INF

    ks_blk = ks.reshape(b, nb, SEL_BLOCK, N_KV_GROUPS, HEAD_DIM).transpose(0, 3, 1, 2, 4)
    vs_blk = vs.reshape(b, nb, SEL_BLOCK, N_KV_GROUPS, HEAD_DIM).transpose(0, 3, 1, 2, 4)
    gather = jax.vmap(jax.vmap(lambda blocks, ids: blocks[ids]))
    kw_pad = jnp.pad(kw, ((0, 0), (WINDOW, 0), (0, 0), (0, 0)))
    vw_pad = jnp.pad(vw, ((0, 0), (WINDOW, 0), (0, 0), (0, 0)))
    n_keys = n_sel * SEL_BLOCK

    def query_block(i):
        t0 = i * Q_BLOCK
        tq = t0 + jnp.arange(Q_BLOCK)
        qc = lax.dynamic_slice_in_dim(q, t0, Q_BLOCK, axis=1)
        ic = lax.dynamic_slice_in_dim(top_idx, t0, Q_BLOCK, axis=2)
        okc = lax.dynamic_slice_in_dim(top_ok, t0, Q_BLOCK, axis=2)
        kg = gather(ks_blk, ic).reshape(b, N_KV_GROUPS, Q_BLOCK, n_keys, HEAD_DIM)
        vg = gather(vs_blk, ic).reshape(b, N_KV_GROUPS, Q_BLOCK, n_keys, HEAD_DIM)
        kpos = ic[..., None] * SEL_BLOCK + jnp.arange(SEL_BLOCK)
        sel_mask = ((kpos <= tq[:, None, None]) & okc[..., None]).reshape(
            b, N_KV_GROUPS, 1, Q_BLOCK, n_keys)
        p = masked_softmax(jnp.einsum('btghd,bgtkd->bghtk', qc, kg), sel_mask)
        o_sel = jnp.einsum('bghtk,bgtkd->btghd', p.astype(vg.dtype), vg)
        kwc = lax.dynamic_slice_in_dim(kw_pad, t0, Q_BLOCK + WINDOW, axis=1)
        vwc = lax.dynamic_slice_in_dim(vw_pad, t0, Q_BLOCK + WINDOW, axis=1)
        wpos = t0 - WINDOW + jnp.arange(Q_BLOCK + WINDOW)
        win_mask = ((wpos[None, :] <= tq[:, None]) & (wpos[None, :] > tq[:, None] - WINDOW) &
                    (wpos[None, :] >= 0))
        p = masked_softmax(jnp.einsum('btghd,bkgd->bghtk', qc, kwc), win_mask)
        o_win = jnp.einsum('bghtk,bkgd->btghd', p.astype(vwc.dtype), vwc)
        return o_sel, o_win

    o_sel, o_win = lax.map(query_block, jnp.arange(s // Q_BLOCK))

    def unblock(o):
        return o.transpose(1, 0, 2, 3, 4, 5).reshape(b, s, N_KV_GROUPS, HEADS_PER_GROUP, HEAD_DIM)

    g = jax.nn.sigmoid(gates.astype(jnp.float32)).astype(q.dtype).reshape(
        b, s, N_KV_GROUPS, HEADS_PER_GROUP, 3)
    o = g[..., 0:1] * o_cmp + g[..., 1:2] * unblock(o_sel) + g[..., 2:3] * unblock(o_win)
    return o.reshape(b, s, ATT_WIDTH)


def short_conv(h, b_gate, c_gate, conv_w, conv_b):
    u = c_gate * h
    s = u.shape[1]
    up = jnp.pad(u, ((0, 0), (CONV_K - 1, 0), (0, 0)))
    v = conv_b
    for k in range(CONV_K):
        v = v + up[:, k:k + s] * conv_w[k]
    return b_gate * v


def hybrid_mixer(h, w_in, cmp_pos, w_cmp1, w_cmp2, conv_w, conv_b, g_out, w_out):
    b, s, _ = h.shape
    z = h @ w_in
    sizes = [Q_COLS] + [KV_COLS] * 6 + [GATE_COLS] + [CONV_WIDTH] * 3
    offs = np.cumsum(sizes)[:-1].tolist()
    q, kc, vc, ks, vs, kw, vw, gates, hc, bg, cg = jnp.split(z, offs, axis=-1)

    def kv(a):
        return a.reshape(b, s, N_KV_GROUPS, HEAD_DIM)

    o_att = nsa_attention(q.reshape(b, s, N_ATT_HEADS, HEAD_DIM), kv(kc), kv(vc), kv(ks), kv(vs),
                          kv(kw), kv(vw), gates.reshape(b, s, N_ATT_HEADS, 3),
                          cmp_pos, w_cmp1, w_cmp2)
    o_conv = short_conv(hc, bg, cg, conv_w, conv_b)
    o = jnp.concatenate([rms_norm(o_att, g_out[:ATT_WIDTH]),
                         rms_norm(o_conv, g_out[ATT_WIDTH:])], axis=-1)
    return o @ w_out


def setup_inputs(seed: int = 0) -> dict:
    key = jax.random.key(seed)
    ks = jax.random.split(key, 16)

    def nrm(k, shape, s):
        return jax.random.normal(k, shape, jnp.float32) * s

    return {
        'x': nrm(ks[0], (BATCH, SEQ, D_MODEL), 1.0),
        'c': nrm(ks[1], (BATCH, D_MODEL), 1.0),
        'w_ada': nrm(ks[2], (DEPTH, D_MODEL, N_SUB * 3 * D_MODEL), D_MODEL ** -0.5),
        'b_ada': nrm(ks[3], (DEPTH, N_SUB * 3 * D_MODEL), 0.01),
        'g_norm': 1.0 + nrm(ks[4], (DEPTH, N_SUB, D_MODEL), 0.02),
        'w_ff_in': nrm(ks[5], (DEPTH, 2, D_MODEL, 2 * D_FF), D_MODEL ** -0.5),
        'w_ff_out': nrm(ks[6], (DEPTH, 2, D_FF, D_MODEL), D_FF ** -0.5),
        'w_mix_in': nrm(ks[7], (DEPTH, D_MODEL, IN_COLS), D_MODEL ** -0.5),
        'cmp_pos': nrm(ks[8], (DEPTH, 2, CMP_BLOCK, HEAD_DIM), 0.1),
        'w_cmp1': nrm(ks[9], (DEPTH, 2, CMP_BLOCK * HEAD_DIM, HEAD_DIM), (CMP_BLOCK * HEAD_DIM) ** -0.5),
        'w_cmp2': nrm(ks[10], (DEPTH, 2, HEAD_DIM, HEAD_DIM), HEAD_DIM ** -0.5),
        'conv_w': nrm(ks[11], (DEPTH, CONV_K, CONV_WIDTH), CONV_K ** -0.5),
        'conv_b': nrm(ks[12], (DEPTH, CONV_WIDTH), 0.01),
        'g_mix_out': 1.0 + nrm(ks[13], (DEPTH, D_MODEL), 0.02),
        'w_mix_out': nrm(ks[14], (DEPTH, D_MODEL, D_MODEL), D_MODEL ** -0.5),
        'g_final': 1.0 + nrm(ks[15], (D_MODEL,), 0.02),
    }


def reference(x, c, w_ada, b_ada, g_norm, w_ff_in, w_ff_out, w_mix_in, cmp_pos, w_cmp1, w_cmp2,
              conv_w, conv_b, g_mix_out, w_mix_out, g_final):
    bsz = c.shape[0]
    c_act = jax.nn.silu(c)
    for l in range(DEPTH):
        mod = (c_act @ w_ada[l] + b_ada[l]).reshape(bsz, N_SUB, 3, D_MODEL)
        h = pre_norm(x, g_norm[l, 0], mod[:, 0, 0], mod[:, 0, 1])
        x = x + 0.5 * mod[:, 0, 2][:, None, :] * swiglu(h, w_ff_in[l, 0], w_ff_out[l, 0])
        h = pre_norm(x, g_norm[l, 1], mod[:, 1, 0], mod[:, 1, 1])
        x = x + mod[:, 1, 2][:, None, :] * hybrid_mixer(
            h, w_mix_in[l], cmp_pos[l], w_cmp1[l], w_cmp2[l], conv_w[l], conv_b[l],
            g_mix_out[l], w_mix_out[l])
        h = pre_norm(x, g_norm[l, 2], mod[:, 2, 0], mod[:, 2, 1])
        x = x + 0.5 * mod[:, 2, 2][:, None, :] * swiglu(h, w_ff_in[l, 1], w_ff_out[l, 1])
    return rms_norm(x, g_final)
```

```python
import functools

import numpy as np
import jax
import jax.numpy as jnp
from jax import lax
from jax.experimental import pallas as pl
from jax.experimental.pallas import tpu as pltpu

F32 = jnp.float32
BF16 = jnp.bfloat16

HEAD_DIM = 64
N_HEADS = 8
N_GROUPS = 2
HEADS_PER_GROUP = N_HEADS // N_GROUPS
ATT_WIDTH = N_HEADS * HEAD_DIM
KV_COLS = N_GROUPS * HEAD_DIM
CONV_K = 3
CMP_BLOCK = 32
CMP_STRIDE = 16
SEL_BLOCK = 64
N_SEL = 8
WINDOW = 512
N_SUB = 3
EPS = 1e-6
NEG_INF = -1e30
FORCE_BONUS = 1e4

LANES = 128
V7X_VMEM_LIMIT = 56 * 1024 * 1024

TOKEN_TILE = 512
Q_TILE = 128
KEY_CHUNK = 128
FF_CHUNKS = ((0, 1536), (1536, 2816))


def _silu(v):
    return v * jax.nn.sigmoid(v)


def _rms(v, g):
    ms = jnp.mean(v * v, axis=-1, keepdims=True)
    return (v * lax.rsqrt(ms + EPS)) * g


def _prenorm(x, g, mod_ref, sub):
    shift = mod_ref[3 * sub:3 * sub + 1, :]
    scale = mod_ref[3 * sub + 1:3 * sub + 2, :]
    return _rms(x, g) * (1.0 + scale) + shift


def _nt_dot(a, b):
    return lax.dot_general(a, b, (((1,), (1,)), ((), ())), preferred_element_type=F32)


def _ada_kernel(c_ref, w_ref, b_ref, o_ref):
    ca = _silu(c_ref[...]).astype(BF16)
    o_ref[...] = jnp.dot(ca, w_ref[...].astype(BF16), preferred_element_type=F32) + b_ref[...]


def _ada_call(c, w_ada, b_ada):
    depth, d, n = w_ada.shape
    bsz = c.shape[0]
    tn = n // 4
    return pl.pallas_call(
        _ada_kernel,
        grid=(depth, n // tn),
        in_specs=[
            pl.BlockSpec((bsz, d), lambda l, j: (0, 0)),
            pl.BlockSpec((None, d, tn), lambda l, j: (l, 0, j)),
            pl.BlockSpec((None, 1, tn), lambda l, j: (l, 0, j)),
        ],
        out_specs=pl.BlockSpec((None, bsz, tn), lambda l, j: (l, 0, j)),
        out_shape=jax.ShapeDtypeStruct((depth, bsz, n), F32),
        compiler_params=pltpu.CompilerParams(
            dimension_semantics=("arbitrary", "arbitrary"), vmem_limit_bytes=V7X_VMEM_LIMIT),
        name="adaln_mod",
    )(c, w_ada, b_ada.reshape(depth, 1, n))


def _ffn_kernel(x_ref, mod_ref, g_ref, win_ref, wout_ref, gf_ref, o_ref, *, sub, d_ff, final):
    x = x_ref[...]
    hb = _prenorm(x, g_ref[...], mod_ref, sub).astype(BF16)
    y = None
    for c0, c1 in FF_CHUNKS:
        gate = jnp.dot(hb, win_ref[:, c0:c1], preferred_element_type=F32)
        up = jnp.dot(hb, win_ref[:, d_ff + c0:d_ff + c1], preferred_element_type=F32)
        act = (_silu(gate) * up).astype(BF16)
        part = jnp.dot(act, wout_ref[c0:c1, :], preferred_element_type=F32)
        y = part if y is None else y + part
    out = x + (0.5 * mod_ref[3 * sub + 2:3 * sub + 3, :]) * y
    if final:
        out = _rms(out, gf_ref[...])
    o_ref[...] = out


def _ffn_call(x, mod, g, w_in, w_out, g_final, *, sub, seq, final):
    t, d = x.shape
    d_ff = w_out.shape[0]
    tiles_per_seq = seq // TOKEN_TILE
    const = lambda i: (0, 0)
    return pl.pallas_call(
        functools.partial(_ffn_kernel, sub=sub, d_ff=d_ff, final=final),
        grid=(t // TOKEN_TILE,),
        in_specs=[
            pl.BlockSpec((TOKEN_TILE, d), lambda i: (i, 0)),
            pl.BlockSpec((None, 3 * N_SUB, d), lambda i: (i // tiles_per_seq, 0, 0)),
            pl.BlockSpec((1, d), const),
            pl.BlockSpec((d, 2 * d_ff), const, pipeline_mode=pl.Buffered(1)),
            pl.BlockSpec((d_ff, d), const, pipeline_mode=pl.Buffered(1)),
            pl.BlockSpec((1, d), const),
        ],
        out_specs=pl.BlockSpec((TOKEN_TILE, d), lambda i: (i, 0)),
        out_shape=jax.ShapeDtypeStruct((t, d), F32),
        compiler_params=pltpu.CompilerParams(
            dimension_semantics=("arbitrary",), vmem_limit_bytes=V7X_VMEM_LIMIT),
        name="ffn_final" if final else "ffn",
    )(x, mod, g, w_in, w_out, g_final)


_Q0, _KC0, _VC0, _KV0, _CV0, _GT0, _IN_P = 0, 512, 640, 768, 1280, 2816, 2944


def _inproj_kernel(x_ref, mod_ref, g_ref, w_ref, q_ref, kc_ref, vc_ref, kv_ref, cv_ref, gt_ref):
    hb = _prenorm(x_ref[...], g_ref[...], mod_ref, 1).astype(BF16)

    def proj(c0, c1):
        return jnp.dot(hb, w_ref[:, c0:c1], preferred_element_type=F32)

    q_ref[...] = proj(_Q0, _KC0).astype(BF16)
    kc_ref[...] = proj(_KC0, _VC0)
    vc_ref[...] = proj(_VC0, _KV0)
    kv_ref[...] = proj(_KV0, _CV0).astype(BF16)
    cv_ref[...] = proj(_CV0, _GT0)
    gt_ref[...] = proj(_GT0, _IN_P)


def _inproj_call(x, mod, g, w_in_p, *, seq):
    t, d = x.shape
    tiles_per_seq = seq // TOKEN_TILE
    const = lambda i: (0, 0)
    row = lambda i: (i, 0)
    widths = (_KC0 - _Q0, KV_COLS, KV_COLS, _CV0 - _KV0, _GT0 - _CV0, _IN_P - _GT0)
    dtypes = (BF16, F32, F32, BF16, F32, F32)
    return pl.pallas_call(
        _inproj_kernel,
        grid=(t // TOKEN_TILE,),
        in_specs=[
            pl.BlockSpec((TOKEN_TILE, d), row),
            pl.BlockSpec((None, 3 * N_SUB, d), lambda i: (i // tiles_per_seq, 0, 0)),
            pl.BlockSpec((1, d), const),
            pl.BlockSpec((d, _IN_P), const, pipeline_mode=pl.Buffered(1)),
        ],
        out_specs=[pl.BlockSpec((TOKEN_TILE, w), row) for w in widths],
        out_shape=[jax.ShapeDtypeStruct((t, w), dt) for w, dt in zip(widths, dtypes)],
        compiler_params=pltpu.CompilerParams(
            dimension_semantics=("arbitrary",), vmem_limit_bytes=V7X_VMEM_LIMIT),
        name="mixer_inproj",
    )(x, mod, g, w_in_p)


def _attn_kernel(q_ref, kc_ref, vc_ref, kv_ref, gt_ref, pos_ref, w1_ref, w2_ref, est_ref, ovt_ref,
                 o_ref, kcmp_ref, vcmp_ref, s_ref, mx_ref, sm_ref, acc_ref, *, seq):
    j = pl.program_id(1)
    tq = Q_TILE
    rows = N_HEADS * tq
    n_cmp = seq // CMP_STRIDE
    per_row = CMP_BLOCK // CMP_STRIDE

    @pl.when(j == 0)
    def _():
        for kvi, (src, dst) in enumerate(((kc_ref, kcmp_ref), (vc_ref, vcmp_ref))):
            r = jnp.concatenate(
                [src[pl.ds(jj, n_cmp, stride=CMP_STRIDE), :] for jj in range(CMP_STRIDE)], axis=1)
            pre = None
            for part in range(per_row):
                xin = (r + pos_ref[kvi, part:part + 1, :]).astype(BF16)
                prod = jnp.dot(xin, w1_ref[kvi, part], preferred_element_type=F32)
                if part:
                    prod = pltpu.roll(prod, n_cmp - part, 0)
                pre = prod if pre is None else pre + prod
            act = _silu(pre).astype(BF16)
            dst[...] = jnp.dot(act, w2_ref[kvi], preferred_element_type=F32).astype(BF16)

    t0 = j * tq
    lane_q = lax.broadcasted_iota(jnp.int32, (tq, LANES), 1)
    qs = []
    for g in range(N_GROUPS):
        keep = (lane_q < HEAD_DIM) if g == 0 else (lane_q >= HEAD_DIM)
        for i in range(HEADS_PER_GROUP):
            qi = q_ref[:, i * LANES:(i + 1) * LANES]
            qs.append(jnp.where(keep, qi, jnp.zeros_like(qi)))
    qst = jnp.concatenate(qs, axis=0)

    row_t = t0 + (lax.broadcasted_iota(jnp.int32, (rows, LANES), 0) & (tq - 1))
    col = lax.broadcasted_iota(jnp.int32, (rows, LANES), 1)

    sc = _nt_dot(qst, kcmp_ref[...])
    cmask = (col * CMP_STRIDE + (CMP_BLOCK - 1)) <= row_t
    sc = jnp.where(cmask, sc, NEG_INF)
    e = jnp.exp(sc - jnp.max(sc, axis=1, keepdims=True))
    p = e / jnp.sum(e, axis=1, keepdims=True)
    p = jnp.where(cmask, p, 0.0)
    pb = p.astype(BF16)
    o_cmp = jnp.dot(pb, vcmp_ref[...], preferred_element_type=F32)

    nb = seq // SEL_BLOCK
    jb = lax.broadcasted_iota(jnp.int32, (nb, tq), 0)
    cur = (t0 + lax.broadcasted_iota(jnp.int32, (nb, tq), 1)) // SEL_BLOCK
    valid = jb <= cur
    forced = ((jb == 0) | (jb == cur) | (jb == cur - 1)).astype(F32)
    bias_rows = []
    for g in range(N_GROUPS):
        imp = None
        for i in range(HEADS_PER_GROUP):
            r0 = (g * HEADS_PER_GROUP + i) * tq
            part = _nt_dot(ovt_ref[...], pb[r0:r0 + tq, :])
            imp = part if imp is None else imp + part
        val = jnp.where(valid, imp + FORCE_BONUS * forced, NEG_INF)
        rank = jnp.zeros((nb, tq), F32)
        for k in range(nb):
            vk = val[k:k + 1, :]
            tie = jnp.where(jb > k, 1.0, 0.0)
            rank = rank + jnp.where(vk > val, 1.0, jnp.where(vk == val, tie, 0.0))
        chosen = jnp.where(valid, jnp.where(rank < N_SEL, 0.0, NEG_INF), NEG_INF)
        bias_t = jnp.concatenate([chosen, jnp.zeros((LANES - nb, tq), F32)], axis=0)
        bias = bias_t.T.astype(BF16)
        bias_rows.extend([bias] * HEADS_PER_GROUP)
    qaug = jnp.concatenate([qst, jnp.concatenate(bias_rows, axis=0)], axis=1)

    mx_ref[...] = jnp.full((rows, LANES), NEG_INF, F32)

    def score_step(c, carry):
        k0 = pl.multiple_of(c * KEY_CHUNK, KEY_CHUNK)
        kaug = jnp.concatenate([kv_ref[pl.ds(k0, KEY_CHUNK), 0:LANES],
                                est_ref[pl.ds(k0, KEY_CHUNK), :]], axis=1)
        s = _nt_dot(qaug, kaug)
        s = jnp.where(k0 + col <= row_t, s, NEG_INF)
        s_ref[c] = s
        mx_ref[...] = jnp.maximum(mx_ref[...], s)
        return carry

    lax.fori_loop(0, j + 1, score_step, 0)
    m_sel = jnp.max(mx_ref[...], axis=1, keepdims=True)
    sm_ref[...] = jnp.zeros((rows, LANES), F32)
    acc_ref[...] = jnp.zeros((rows, LANES), F32)

    def value_step(c, carry):
        k0 = pl.multiple_of(c * KEY_CHUNK, KEY_CHUNK)
        pe = jnp.exp(s_ref[c] - m_sel)
        sm_ref[...] = sm_ref[...] + pe
        acc_ref[...] = acc_ref[...] + jnp.dot(
            pe.astype(BF16), kv_ref[pl.ds(k0, KEY_CHUNK), LANES:2 * LANES], preferred_element_type=F32)
        return carry

    lax.fori_loop(0, j + 1, value_step, 0)
    o_sel = acc_ref[...] / jnp.sum(sm_ref[...], axis=1, keepdims=True)

    span = WINDOW + tq
    w0 = pl.multiple_of(jnp.maximum(t0 - WINDOW, 0), tq)
    kw = kv_ref[pl.ds(w0, span), 2 * LANES:3 * LANES]
    vw = kv_ref[pl.ds(w0, span), 3 * LANES:4 * LANES]
    sw = _nt_dot(qst, kw)
    row_w = t0 + (lax.broadcasted_iota(jnp.int32, (rows, span), 0) & (tq - 1))
    kpos = w0 + lax.broadcasted_iota(jnp.int32, (rows, span), 1)
    sw = jnp.where(kpos <= row_w, jnp.where(kpos > row_w - WINDOW, sw, NEG_INF), NEG_INF)
    ew = jnp.exp(sw - jnp.max(sw, axis=1, keepdims=True))
    o_win = jnp.dot(ew.astype(BF16), vw, preferred_element_type=F32) / jnp.sum(ew, axis=1, keepdims=True)

    sg = jax.nn.sigmoid(gt_ref[...])
    for i in range(HEADS_PER_GROUP):
        halves = []
        for g in range(N_GROUPS):
            r0 = (g * HEADS_PER_GROUP + i) * tq
            tot = None
            for br, ob in enumerate((o_cmp, o_sel, o_win)):
                cix = br * N_HEADS + g * HEADS_PER_GROUP + i
                term = sg[:, cix:cix + 1] * ob[r0:r0 + tq, :]
                tot = term if tot is None else tot + term
            halves.append(tot)
        o_ref[:, i * LANES:(i + 1) * LANES] = jnp.where(lane_q < HEAD_DIM, halves[0], halves[1])


def _attn_call(q, kc, vc, kv, gt, pos_r, w1_e, w2_e, est, ovt, *, bsz, seq):
    nq = seq // Q_TILE
    rows = N_HEADS * Q_TILE
    n_cmp = seq // CMP_STRIDE
    tile = lambda b, j: (b * nq + j, 0)
    per_seq = lambda b, j: (b, 0, 0)
    c2 = lambda b, j: (0, 0)
    c3 = lambda b, j: (0, 0, 0)
    c4 = lambda b, j: (0, 0, 0, 0)
    return pl.pallas_call(
        functools.partial(_attn_kernel, seq=seq),
        grid=(bsz, nq),
        in_specs=[
            pl.BlockSpec((Q_TILE, ATT_WIDTH), tile),
            pl.BlockSpec((None, seq, KV_COLS), per_seq),
            pl.BlockSpec((None, seq, KV_COLS), per_seq),
            pl.BlockSpec((None, seq, 4 * KV_COLS), per_seq),
            pl.BlockSpec((Q_TILE, LANES), tile),
            pl.BlockSpec(pos_r.shape, c3),
            pl.BlockSpec(w1_e.shape, c4),
            pl.BlockSpec(w2_e.shape, c3),
            pl.BlockSpec(est.shape, c2),
            pl.BlockSpec(ovt.shape, c2),
        ],
        out_specs=pl.BlockSpec((Q_TILE, ATT_WIDTH), tile),
        out_shape=jax.ShapeDtypeStruct((bsz * seq, ATT_WIDTH), F32),
        scratch_shapes=[
            pltpu.VMEM((n_cmp, LANES), BF16),
            pltpu.VMEM((n_cmp, LANES), BF16),
            pltpu.VMEM((seq // KEY_CHUNK, rows, KEY_CHUNK), F32),
            pltpu.VMEM((rows, LANES), F32),
            pltpu.VMEM((rows, LANES), F32),
            pltpu.VMEM((rows, LANES), F32),
        ],
        compiler_params=pltpu.CompilerParams(
            dimension_semantics=("arbitrary", "arbitrary"), vmem_limit_bytes=V7X_VMEM_LIMIT),
        name="nsa_attention",
    )(q, kc, vc, kv, gt, pos_r, w1_e, w2_e, est, ovt)


def _outproj_kernel(oa_ref, cv_ref, halo_ref, x_ref, mod_ref, cw_ref, cb_ref, go_ref, w_ref, o_ref,
                    *, tiles_per_seq):
    i = pl.program_id(0)
    cw = ATT_WIDTH
    hc = cv_ref[:, 0:cw]
    bg = cv_ref[:, cw:2 * cw]
    cg = cv_ref[:, 2 * cw:3 * cw]
    u = cg * hc
    hu = halo_ref[:, 2 * cw:3 * cw] * halo_ref[:, 0:cw]
    hu = jnp.where(i % tiles_per_seq == 0, 0.0, hu)
    row = lax.broadcasted_iota(jnp.int32, u.shape, 0)
    u1 = jnp.where(row == 0, hu[7:8, :], pltpu.roll(u, 1, 0))
    u2 = jnp.where(row == 0, hu[6:7, :], jnp.where(row == 1, hu[7:8, :], pltpu.roll(u, 2, 0)))
    v = cb_ref[...] + u2 * cw_ref[0:1, :]
    v = v + u1 * cw_ref[1:2, :]
    v = v + u * cw_ref[2:3, :]
    oc = bg * v
    oa = _rms(oa_ref[...], go_ref[:, 0:cw])
    oc = _rms(oc, go_ref[:, cw:2 * cw])
    ob = jnp.concatenate([oa, oc], axis=1).astype(BF16)
    y = jnp.dot(ob, w_ref[...], preferred_element_type=F32)
    o_ref[...] = x_ref[...] + mod_ref[5:6, :] * y


def _outproj_call(oa, cv, x, mod, conv_w, conv_b, g_out_p, w_out_p, *, seq):
    t, d = x.shape
    tiles_per_seq = seq // TOKEN_TILE
    halo_blocks = TOKEN_TILE // 8
    const = lambda i: (0, 0)
    row = lambda i: (i, 0)
    return pl.pallas_call(
        functools.partial(_outproj_kernel, tiles_per_seq=tiles_per_seq),
        grid=(t // TOKEN_TILE,),
        in_specs=[
            pl.BlockSpec((TOKEN_TILE, oa.shape[1]), row),
            pl.BlockSpec((TOKEN_TILE, cv.shape[1]), row),
            pl.BlockSpec((8, cv.shape[1]), lambda i: (jnp.maximum(i * halo_blocks - 1, 0), 0)),
            pl.BlockSpec((TOKEN_TILE, d), row),
            pl.BlockSpec((None, 3 * N_SUB, d), lambda i: (i // tiles_per_seq, 0, 0)),
            pl.BlockSpec(conv_w.shape, const),
            pl.BlockSpec(conv_b.shape, const),
            pl.BlockSpec(g_out_p.shape, const),
            pl.BlockSpec(w_out_p.shape, const, pipeline_mode=pl.Buffered(1)),
        ],
        out_specs=pl.BlockSpec((TOKEN_TILE, d), row),
        out_shape=jax.ShapeDtypeStruct((t, d), F32),
        compiler_params=pltpu.CompilerParams(
            dimension_semantics=("arbitrary",), vmem_limit_bytes=V7X_VMEM_LIMIT),
        name="mixer_outproj",
    )(oa, cv, cv, x, mod, conv_w, conv_b, g_out_p, w_out_p)


def _head_perm():
    idx = np.empty(ATT_WIDTH, np.int32)
    for i in range(HEADS_PER_GROUP):
        for g in range(N_GROUPS):
            h = g * HEADS_PER_GROUP + i
            dst = i * LANES + g * HEAD_DIM
            idx[dst:dst + HEAD_DIM] = np.arange(h * HEAD_DIM, (h + 1) * HEAD_DIM)
    return idx


def _gate_perm():
    return np.array([h * 3 + br for br in range(3) for h in range(N_HEADS)], np.int32)


def _prep_mixer_in(w):
    offs = np.cumsum([0, ATT_WIDTH] + [KV_COLS] * 6 + [3 * N_HEADS] + [ATT_WIDTH] * 3)
    q = w[..., offs[0]:offs[1]][..., _head_perm()] * (HEAD_DIM ** -0.5)
    kvs = w[..., offs[1]:offs[7]]
    gates = w[..., offs[7]:offs[8]][..., _gate_perm()]
    conv = w[..., offs[8]:offs[11]]
    pad = jnp.zeros(w.shape[:-1] + (LANES - 3 * N_HEADS,), w.dtype)
    return jnp.concatenate([q, kvs, conv, gates, pad], axis=-1).astype(BF16)


def _prep_compress(cmp_pos, w_cmp1, w_cmp2):
    depth = cmp_pos.shape[0]
    eye = jnp.eye(N_GROUPS, dtype=F32)
    pos = cmp_pos.reshape(depth, 2, CMP_BLOCK // CMP_STRIDE, CMP_STRIDE, 1, HEAD_DIM)
    pos_r = jnp.broadcast_to(pos, pos.shape[:4] + (N_GROUPS, HEAD_DIM)).reshape(
        depth, 2, CMP_BLOCK // CMP_STRIDE, CMP_STRIDE * LANES)
    w1 = w_cmp1.reshape(depth, 2, CMP_BLOCK // CMP_STRIDE, CMP_STRIDE, HEAD_DIM, HEAD_DIM)
    w1_e = jnp.einsum("lkpjde,gh->lkpjgdhe", w1, eye).reshape(
        depth, 2, CMP_BLOCK // CMP_STRIDE, CMP_STRIDE * LANES, LANES).astype(BF16)
    w2_e = jnp.einsum("lkde,gh->lkgdhe", w_cmp2, eye).reshape(depth, 2, LANES, LANES).astype(BF16)
    return pos_r, w1_e, w2_e


def _selection_constants(seq):
    n_cmp = seq // CMP_STRIDE
    nb = seq // SEL_BLOCK
    est = (np.arange(seq)[:, None] // SEL_BLOCK == np.arange(LANES)[None, :]).astype(np.float32)
    cs = np.arange(n_cmp) * CMP_STRIDE
    js = np.arange(nb) * SEL_BLOCK
    ovt = ((cs[None, :] < js[:, None] + SEL_BLOCK) & (cs[None, :] + CMP_BLOCK > js[:, None])).astype(np.float32)
    return jnp.asarray(est, BF16), jnp.asarray(ovt, BF16)


def kernel(x, c, w_ada, b_ada, g_norm, w_ff_in, w_ff_out, w_mix_in, cmp_pos, w_cmp1, w_cmp2,
           conv_w, conv_b, g_mix_out, w_mix_out, g_final):
    bsz, seq, d = x.shape
    depth = w_ada.shape[0]
    assert seq % TOKEN_TILE == 0 and seq % Q_TILE == 0 and seq // CMP_STRIDE == LANES
    assert w_mix_out.shape[1] == 2 * ATT_WIDTH == d

    perm = _head_perm()
    w_ff_in_b = w_ff_in.astype(BF16)
    w_ff_out_b = w_ff_out.astype(BF16)
    w_in_p = _prep_mixer_in(w_mix_in)
    row_perm = np.concatenate([perm, np.arange(ATT_WIDTH, d)])
    w_out_p = w_mix_out[:, row_perm, :].astype(BF16)
    g_out_p = g_mix_out[:, row_perm]
    pos_r, w1_e, w2_e = _prep_compress(cmp_pos, w_cmp1, w_cmp2)
    est, ovt = _selection_constants(seq)

    mod_all = _ada_call(c, w_ada, b_ada).reshape(depth, bsz, 3 * N_SUB, d)
    gf = g_final.reshape(1, d)

    xt = x.reshape(bsz * seq, d)
    for l in range(depth):
        mod = mod_all[l]
        xt = _ffn_call(xt, mod, g_norm[l, 0:1], w_ff_in_b[l, 0], w_ff_out_b[l, 0], gf,
                       sub=0, seq=seq, final=False)
        q, kc, vc, kv, cv, gt = _inproj_call(xt, mod, g_norm[l, 1:2], w_in_p[l], seq=seq)
        oa = _attn_call(q, kc.reshape(bsz, seq, KV_COLS), vc.reshape(bsz, seq, KV_COLS),
                        kv.reshape(bsz, seq, 4 * KV_COLS), gt, pos_r[l], w1_e[l], w2_e[l], est, ovt,
                        bsz=bsz, seq=seq)
        xt = _outproj_call(oa, cv, xt, mod, conv_w[l], conv_b[l:l + 1], g_out_p[l:l + 1], w_out_p[l],
                           seq=seq)
        xt = _ffn_call(xt, mod, g_norm[l, 2:3], w_ff_in_b[l, 1], w_ff_out_b[l, 1], gf,
                       sub=2, seq=seq, final=(l == depth - 1))
    return xt.reshape(bsz, seq, d)
```

```python
import functools
import math

import numpy as np
import jax
import jax.numpy as jnp
from jax import lax
from jax.experimental import pallas as pl
from jax.experimental.pallas import tpu as pltpu

F32 = jnp.float32
BF16 = jnp.bfloat16

HEAD_DIM = 64
N_HEADS = 8
N_GROUPS = 2
HEADS_PER_GROUP = N_HEADS // N_GROUPS
ATT_WIDTH = N_HEADS * HEAD_DIM
KV_COLS = N_GROUPS * HEAD_DIM
CONV_K = 3
CMP_BLOCK = 32
CMP_STRIDE = 16
SEL_BLOCK = 64
N_SEL = 8
WINDOW = 512
N_SUB = 3
EPS = 1e-6
NEG_INF = -1e30
FORCE_BONUS = 1e4
LOG2_E = math.log2(math.e)

LANES = 128
V7X_VMEM_LIMIT = 56 * 1024 * 1024

TOKEN_TILE = 512
Q_TILE = 256
KEY_CHUNK = 256
FF_CHUNKS = ((0, 1536), (1536, 2816))


def _silu(v):
    return v * jax.nn.sigmoid(v)


def _rms(v, g):
    ms = jnp.mean(v * v, axis=-1, keepdims=True)
    return (v * lax.rsqrt(ms + EPS)) * g


def _prenorm(x, g, mod_ref, sub):
    shift = mod_ref[3 * sub:3 * sub + 1, :]
    scale = mod_ref[3 * sub + 1:3 * sub + 2, :]
    return _rms(x, g) * (1.0 + scale) + shift


def _nt_dot(a, b):
    return lax.dot_general(a, b, (((1,), (1,)), ((), ())), preferred_element_type=F32)


def _ada_kernel(c_ref, w_ref, b_ref, o_ref):
    ca = _silu(c_ref[...]).astype(BF16)
    o_ref[...] = jnp.dot(ca, w_ref[...].astype(BF16), preferred_element_type=F32) + b_ref[...]


def _ada_call(c, w_ada, b_ada):
    depth, d, n = w_ada.shape
    bsz = c.shape[0]
    tn = n // 4
    return pl.pallas_call(
        _ada_kernel,
        grid=(depth, n // tn),
        in_specs=[
            pl.BlockSpec((bsz, d), lambda l, j: (0, 0)),
            pl.BlockSpec((None, d, tn), lambda l, j: (l, 0, j)),
            pl.BlockSpec((None, 1, tn), lambda l, j: (l, 0, j)),
        ],
        out_specs=pl.BlockSpec((None, bsz, tn), lambda l, j: (l, 0, j)),
        out_shape=jax.ShapeDtypeStruct((depth, bsz, n), F32),
        compiler_params=pltpu.CompilerParams(
            dimension_semantics=("arbitrary", "arbitrary"), vmem_limit_bytes=V7X_VMEM_LIMIT),
        name="adaln_mod",
    )(c, w_ada, b_ada.reshape(depth, 1, n))


def _ffn_kernel(x_ref, mod_ref, g_ref, win_ref, wout_ref, gf_ref, o_ref, *, sub, d_ff, final):
    x = x_ref[...]
    hb = _prenorm(x, g_ref[...], mod_ref, sub).astype(BF16)
    y = None
    for c0, c1 in FF_CHUNKS:
        gate = jnp.dot(hb, win_ref[:, c0:c1], preferred_element_type=F32)
        up = jnp.dot(hb, win_ref[:, d_ff + c0:d_ff + c1], preferred_element_type=F32)
        act = (_silu(gate) * up).astype(BF16)
        part = jnp.dot(act, wout_ref[c0:c1, :], preferred_element_type=F32)
        y = part if y is None else y + part
    out = x + (0.5 * mod_ref[3 * sub + 2:3 * sub + 3, :]) * y
    if final:
        out = _rms(out, gf_ref[...])
    o_ref[...] = out


def _ffn_call(x, mod, g, w_in, w_out, g_final, *, sub, seq, final):
    t, d = x.shape
    d_ff = w_out.shape[0]
    tiles_per_seq = seq // TOKEN_TILE
    const = lambda i: (0, 0)
    return pl.pallas_call(
        functools.partial(_ffn_kernel, sub=sub, d_ff=d_ff, final=final),
        grid=(t // TOKEN_TILE,),
        in_specs=[
            pl.BlockSpec((TOKEN_TILE, d), lambda i: (i, 0)),
            pl.BlockSpec((None, 3 * N_SUB, d), lambda i: (i // tiles_per_seq, 0, 0)),
            pl.BlockSpec((1, d), const),
            pl.BlockSpec((d, 2 * d_ff), const, pipeline_mode=pl.Buffered(1)),
            pl.BlockSpec((d_ff, d), const, pipeline_mode=pl.Buffered(1)),
            pl.BlockSpec((1, d), const),
        ],
        out_specs=pl.BlockSpec((TOKEN_TILE, d), lambda i: (i, 0)),
        out_shape=jax.ShapeDtypeStruct((t, d), F32),
        compiler_params=pltpu.CompilerParams(
            dimension_semantics=("arbitrary",), vmem_limit_bytes=V7X_VMEM_LIMIT),
        name="ffn_final" if final else "ffn",
    )(x, mod, g, w_in, w_out, g_final)


_Q0, _KC0, _VC0, _KV0, _CV0, _GT0, _IN_P = 0, 512, 640, 768, 1280, 2816, 2944
_KS, _VS, _KW, _VW = 0, LANES, 2 * LANES, 3 * LANES


def _inproj_kernel(x_ref, mod_ref, g_ref, w_ref, q_ref, kc_ref, vc_ref, kv_ref, cv_ref, gt_ref):
    hb = _prenorm(x_ref[...], g_ref[...], mod_ref, 1).astype(BF16)

    def proj(c0, c1):
        return jnp.dot(hb, w_ref[:, c0:c1], preferred_element_type=F32)

    q_ref[...] = proj(_Q0, _KC0).astype(BF16)
    kc_ref[...] = proj(_KC0, _VC0)
    vc_ref[...] = proj(_VC0, _KV0)
    kv_ref[...] = proj(_KV0, _CV0).astype(BF16)
    cv_ref[...] = proj(_CV0, _GT0)
    gt_ref[...] = proj(_GT0, _IN_P)


def _inproj_call(x, mod, g, w_in_p, *, seq):
    t, d = x.shape
    tiles_per_seq = seq // TOKEN_TILE
    const = lambda i: (0, 0)
    row = lambda i: (i, 0)
    widths = (_KC0 - _Q0, KV_COLS, KV_COLS, _CV0 - _KV0, _GT0 - _CV0, _IN_P - _GT0)
    dtypes = (BF16, F32, F32, BF16, F32, F32)
    return pl.pallas_call(
        _inproj_kernel,
        grid=(t // TOKEN_TILE,),
        in_specs=[
            pl.BlockSpec((TOKEN_TILE, d), row),
            pl.BlockSpec((None, 3 * N_SUB, d), lambda i: (i // tiles_per_seq, 0, 0)),
            pl.BlockSpec((1, d), const),
            pl.BlockSpec((d, _IN_P), const, pipeline_mode=pl.Buffered(1)),
        ],
        out_specs=[pl.BlockSpec((TOKEN_TILE, w), row) for w in widths],
        out_shape=[jax.ShapeDtypeStruct((t, w), dt) for w, dt in zip(widths, dtypes)],
        compiler_params=pltpu.CompilerParams(
            dimension_semantics=("arbitrary",), vmem_limit_bytes=V7X_VMEM_LIMIT),
        name="mixer_inproj",
    )(x, mod, g, w_in_p)


def _attn_kernel(q_ref, kc_ref, vc_ref, kv_ref, gt_ref, pos_ref, w1_ref, w2_ref, est_ref, ovt_ref,
                 o_ref, kcmp_ref, vcmp_ref, s_ref, mx_ref, acc_ref, *, seq):
    j = pl.program_id(1)
    tq = Q_TILE
    ck = KEY_CHUNK
    rows = HEADS_PER_GROUP * tq
    n_cmp = seq // CMP_STRIDE
    per_row = CMP_BLOCK // CMP_STRIDE
    nb = seq // SEL_BLOCK

    @pl.when(j == 0)
    def _():
        for kvi, (src, dst) in enumerate(((kc_ref, kcmp_ref), (vc_ref, vcmp_ref))):
            r = jnp.concatenate(
                [src[pl.ds(jj, n_cmp, stride=CMP_STRIDE), :] for jj in range(CMP_STRIDE)], axis=1)
            pre = None
            for part in range(per_row):
                xin = (r + pos_ref[kvi, part:part + 1, :]).astype(BF16)
                prod = jnp.dot(xin, w1_ref[kvi, part], preferred_element_type=F32)
                if part:
                    prod = pltpu.roll(prod, n_cmp - part, 0)
                pre = prod if pre is None else pre + prod
            act = _silu(pre).astype(BF16)
            dst[...] = jnp.dot(act, w2_ref[kvi], preferred_element_type=F32).astype(BF16)

    t0 = j * tq
    lane_q = lax.broadcasted_iota(jnp.int32, (tq, LANES), 1)
    q_loc = lax.broadcasted_iota(jnp.int32, (rows, ck), 0) & (tq - 1)
    key_loc = lax.broadcasted_iota(jnp.int32, (rows, ck), 1)
    causal = key_loc <= q_loc
    win_tail = key_loc > q_loc
    q_loc_c = lax.broadcasted_iota(jnp.int32, (rows, n_cmp), 0) & (tq - 1)
    cmask = (lax.broadcasted_iota(jnp.int32, (rows, n_cmp), 1) * CMP_STRIDE + (CMP_BLOCK - 1)) <= t0 + q_loc_c
    sg = jax.nn.sigmoid(gt_ref[...])

    jb = lax.broadcasted_iota(jnp.int32, (nb, tq), 0)
    cur = (t0 + lax.broadcasted_iota(jnp.int32, (nb, tq), 1)) // SEL_BLOCK
    valid = jb <= cur
    forced = ((jb == 0) | (jb == cur) | (jb == cur - 1)).astype(F32)

    def kv_rows(col, c0, n):
        return kv_ref[pl.ds(pl.multiple_of(c0 * ck, ck), n * ck), col:col + LANES]

    def sel_keys(c0, n):
        est = est_ref[pl.ds(pl.multiple_of(c0 * ck, ck), n * ck), :]
        return jnp.concatenate([kv_rows(_KS, c0, n), est], axis=1)

    def with_ones(v):
        return jnp.concatenate([v, jnp.ones(v.shape, BF16)], axis=1)

    def store_scores(s, c0, n):
        smax = None
        for k in range(n):
            s_ref[c0 + k] = s[:, k * ck:(k + 1) * ck]
            for h in range(ck // LANES):
                part = s[:, k * ck + h * LANES:k * ck + (h + 1) * LANES]
                smax = part if smax is None else jnp.maximum(smax, part)
        mx_ref[...] = jnp.maximum(mx_ref[...], smax)

    def probs(c0, n, m):
        s = s_ref[c0] if n == 1 else jnp.concatenate([s_ref[c0 + k] for k in range(n)], axis=1)
        return jnp.exp2(s - m).astype(BF16)

    def normalised(ov):
        return ov[:, :LANES] / ov[:, LANES:]

    n_pairs = lax.shift_right_logical(j, 1)
    odd = (j & 1) == 1
    win_a = jnp.maximum(j - 2, 0)
    win_b = jnp.maximum(j - 1, 0)
    off_a = jnp.where(j >= 2, 0.0, NEG_INF)
    off_b = jnp.where(j >= 1, 0.0, NEG_INF)

    for g in range(N_GROUPS):
        keep = (lane_q < HEAD_DIM) if g == 0 else (lane_q >= HEAD_DIM)
        qs = []
        for i in range(HEADS_PER_GROUP):
            qi = q_ref[:, i * LANES:(i + 1) * LANES]
            qs.append(jnp.where(keep, qi, jnp.zeros_like(qi)))
        qg = jnp.concatenate(qs, axis=0)

        sc = jnp.where(cmask, _nt_dot(qg, kcmp_ref[...]), NEG_INF)
        e = jnp.exp2(sc - jnp.max(sc, axis=1, keepdims=True))
        p = e / jnp.sum(e, axis=1, keepdims=True)
        pb = jnp.where(cmask, p, 0.0).astype(BF16)
        o_cmp = jnp.dot(pb, vcmp_ref[...], preferred_element_type=F32)

        imp = None
        for i in range(HEADS_PER_GROUP):
            part = _nt_dot(ovt_ref[...], pb[i * tq:(i + 1) * tq, :])
            imp = part if imp is None else imp + part
        val = jnp.where(valid, imp + FORCE_BONUS * forced, NEG_INF)
        rank = jnp.zeros((nb, tq), F32)
        for k in range(nb):
            vk = val[k:k + 1, :]
            tie = jnp.where(jb > k, 1.0, 0.0)
            rank = rank + jnp.where(vk > val, 1.0, jnp.where(vk == val, tie, 0.0))
        chosen = jnp.where(valid, jnp.where(rank < N_SEL, 0.0, NEG_INF), NEG_INF)
        bias_t = jnp.concatenate([chosen, jnp.zeros((LANES - nb, tq), F32)], axis=0)
        bias = bias_t.T.astype(BF16)
        qaug = jnp.concatenate([qg, jnp.concatenate([bias] * HEADS_PER_GROUP, axis=0)], axis=1)

        s_d = jnp.where(causal, _nt_dot(qaug, sel_keys(j, 1)), NEG_INF)
        s_ref[j] = s_d
        mx_ref[...] = jnp.maximum(s_d[:, :LANES], s_d[:, LANES:])

        def pair_score(u, carry):
            store_scores(_nt_dot(qaug, sel_keys(2 * u, 2)), 2 * u, 2)
            return carry

        lax.fori_loop(0, n_pairs, pair_score, 0)

        @pl.when(odd)
        def _():
            store_scores(_nt_dot(qaug, sel_keys(j - 1, 1)), j - 1, 1)

        m_sel = jnp.max(mx_ref[...], axis=1, keepdims=True)
        acc_ref[...] = jnp.dot(probs(j, 1, m_sel), with_ones(kv_rows(_VS, j, 1)),
                               preferred_element_type=F32)

        def pair_value(u, carry):
            acc_ref[...] = acc_ref[...] + jnp.dot(
                probs(2 * u, 2, m_sel), with_ones(kv_rows(_VS, 2 * u, 2)), preferred_element_type=F32)
            return carry

        lax.fori_loop(0, n_pairs, pair_value, 0)

        @pl.when(odd)
        def _():
            acc_ref[...] = acc_ref[...] + jnp.dot(
                probs(j - 1, 1, m_sel), with_ones(kv_rows(_VS, j - 1, 1)), preferred_element_type=F32)

        o_sel = normalised(acc_ref[...])

        kw = jnp.concatenate([kv_rows(_KW, win_a, 1), kv_rows(_KW, win_b, 1), kv_rows(_KW, j, 1)], axis=0)
        vw = jnp.concatenate([kv_rows(_VW, win_a, 1), kv_rows(_VW, win_b, 1), kv_rows(_VW, j, 1)], axis=0)
        sw = _nt_dot(qg, kw)
        sw = jnp.concatenate([jnp.where(win_tail, sw[:, :ck] + off_a, NEG_INF),
                              sw[:, ck:2 * ck] + off_b,
                              jnp.where(causal, sw[:, 2 * ck:], NEG_INF)], axis=1)
        pw = jnp.exp2(sw - jnp.max(sw, axis=1, keepdims=True)).astype(BF16)
        vw1 = with_ones(vw)
        half = rows // 2
        o_win = normalised(jnp.concatenate(
            [jnp.dot(pw[:half], vw1, preferred_element_type=F32),
             jnp.dot(pw[half:], vw1, preferred_element_type=F32)], axis=0))

        for i in range(HEADS_PER_GROUP):
            tot = None
            for br, ob in enumerate((o_cmp, o_sel, o_win)):
                cix = br * N_HEADS + g * HEADS_PER_GROUP + i
                term = sg[:, cix:cix + 1] * ob[i * tq:(i + 1) * tq, :]
                tot = term if tot is None else tot + term
            cols = slice(i * LANES, (i + 1) * LANES)
            if g == 0:
                o_ref[:, cols] = tot
            else:
                o_ref[:, cols] = jnp.where(lane_q < HEAD_DIM, o_ref[:, cols], tot)


def _attn_call(q, kc, vc, kv, gt, pos_r, w1_e, w2_e, est, ovt, *, bsz, seq):
    assert WINDOW == 2 * KEY_CHUNK and Q_TILE == KEY_CHUNK
    nq = seq // Q_TILE
    rows = HEADS_PER_GROUP * Q_TILE
    n_cmp = seq // CMP_STRIDE
    tile = lambda b, j: (b * nq + j, 0)
    per_seq = lambda b, j: (b, 0, 0)
    c2 = lambda b, j: (0, 0)
    c3 = lambda b, j: (0, 0, 0)
    c4 = lambda b, j: (0, 0, 0, 0)
    return pl.pallas_call(
        functools.partial(_attn_kernel, seq=seq),
        grid=(bsz, nq),
        in_specs=[
            pl.BlockSpec((Q_TILE, ATT_WIDTH), tile),
            pl.BlockSpec((None, seq, KV_COLS), per_seq),
            pl.BlockSpec((None, seq, KV_COLS), per_seq),
            pl.BlockSpec((None, seq, 4 * KV_COLS), per_seq),
            pl.BlockSpec((Q_TILE, LANES), tile),
            pl.BlockSpec(pos_r.shape, c3),
            pl.BlockSpec(w1_e.shape, c4),
            pl.BlockSpec(w2_e.shape, c3),
            pl.BlockSpec(est.shape, c2),
            pl.BlockSpec(ovt.shape, c2),
        ],
        out_specs=pl.BlockSpec((Q_TILE, ATT_WIDTH), tile),
        out_shape=jax.ShapeDtypeStruct((bsz * seq, ATT_WIDTH), F32),
        scratch_shapes=[
            pltpu.VMEM((n_cmp, LANES), BF16),
            pltpu.VMEM((n_cmp, LANES), BF16),
            pltpu.VMEM((seq // KEY_CHUNK, rows, KEY_CHUNK), F32),
            pltpu.VMEM((rows, LANES), F32),
            pltpu.VMEM((rows, 2 * LANES), F32),
        ],
        compiler_params=pltpu.CompilerParams(
            dimension_semantics=("arbitrary", "arbitrary"), vmem_limit_bytes=V7X_VMEM_LIMIT),
        name="nsa_attention",
    )(q, kc, vc, kv, gt, pos_r, w1_e, w2_e, est, ovt)


def _outproj_kernel(oa_ref, cv_ref, halo_ref, x_ref, mod_ref, cw_ref, cb_ref, go_ref, w_ref, o_ref,
                    *, tiles_per_seq):
    i = pl.program_id(0)
    cw = ATT_WIDTH
    hc = cv_ref[:, 0:cw]
    bg = cv_ref[:, cw:2 * cw]
    cg = cv_ref[:, 2 * cw:3 * cw]
    u = cg * hc
    hu = halo_ref[:, 2 * cw:3 * cw] * halo_ref[:, 0:cw]
    hu = jnp.where(i % tiles_per_seq == 0, 0.0, hu)
    row = lax.broadcasted_iota(jnp.int32, u.shape, 0)
    u1 = jnp.where(row == 0, hu[7:8, :], pltpu.roll(u, 1, 0))
    u2 = jnp.where(row == 0, hu[6:7, :], jnp.where(row == 1, hu[7:8, :], pltpu.roll(u, 2, 0)))
    v = cb_ref[...] + u2 * cw_ref[0:1, :]
    v = v + u1 * cw_ref[1:2, :]
    v = v + u * cw_ref[2:3, :]
    oc = bg * v
    oa = _rms(oa_ref[...], go_ref[:, 0:cw])
    oc = _rms(oc, go_ref[:, cw:2 * cw])
    ob = jnp.concatenate([oa, oc], axis=1).astype(BF16)
    y = jnp.dot(ob, w_ref[...], preferred_element_type=F32)
    o_ref[...] = x_ref[...] + mod_ref[5:6, :] * y


def _outproj_call(oa, cv, x, mod, conv_w, conv_b, g_out_p, w_out_p, *, seq):
    t, d = x.shape
    tiles_per_seq = seq // TOKEN_TILE
    halo_blocks = TOKEN_TILE // 8
    const = lambda i: (0, 0)
    row = lambda i: (i, 0)
    return pl.pallas_call(
        functools.partial(_outproj_kernel, tiles_per_seq=tiles_per_seq),
        grid=(t // TOKEN_TILE,),
        in_specs=[
            pl.BlockSpec((TOKEN_TILE, oa.shape[1]), row),
            pl.BlockSpec((TOKEN_TILE, cv.shape[1]), row),
            pl.BlockSpec((8, cv.shape[1]), lambda i: (jnp.maximum(i * halo_blocks - 1, 0), 0)),
            pl.BlockSpec((TOKEN_TILE, d), row),
            pl.BlockSpec((None, 3 * N_SUB, d), lambda i: (i // tiles_per_seq, 0, 0)),
            pl.BlockSpec(conv_w.shape, const),
            pl.BlockSpec(conv_b.shape, const),
            pl.BlockSpec(g_out_p.shape, const),
            pl.BlockSpec(w_out_p.shape, const, pipeline_mode=pl.Buffered(1)),
        ],
        out_specs=pl.BlockSpec((TOKEN_TILE, d), row),
        out_shape=jax.ShapeDtypeStruct((t, d), F32),
        compiler_params=pltpu.CompilerParams(
            dimension_semantics=("arbitrary",), vmem_limit_bytes=V7X_VMEM_LIMIT),
        name="mixer_outproj",
    )(oa, cv, cv, x, mod, conv_w, conv_b, g_out_p, w_out_p)


def _head_perm():
    idx = np.empty(ATT_WIDTH, np.int32)
    for i in range(HEADS_PER_GROUP):
        for g in range(N_GROUPS):
            h = g * HEADS_PER_GROUP + i
            dst = i * LANES + g * HEAD_DIM
            idx[dst:dst + HEAD_DIM] = np.arange(h * HEAD_DIM, (h + 1) * HEAD_DIM)
    return idx


def _gate_perm():
    return np.array([h * 3 + br for br in range(3) for h in range(N_HEADS)], np.int32)


def _prep_mixer_in(w):
    offs = np.cumsum([0, ATT_WIDTH] + [KV_COLS] * 6 + [3 * N_HEADS] + [ATT_WIDTH] * 3)
    q = w[..., offs[0]:offs[1]][..., _head_perm()] * (LOG2_E * HEAD_DIM ** -0.5)
    kvs = w[..., offs[1]:offs[7]]
    gates = w[..., offs[7]:offs[8]][..., _gate_perm()]
    conv = w[..., offs[8]:offs[11]]
    pad = jnp.zeros(w.shape[:-1] + (LANES - 3 * N_HEADS,), w.dtype)
    return jnp.concatenate([q, kvs, conv, gates, pad], axis=-1).astype(BF16)


def _prep_compress(cmp_pos, w_cmp1, w_cmp2):
    depth = cmp_pos.shape[0]
    eye = jnp.eye(N_GROUPS, dtype=F32)
    pos = cmp_pos.reshape(depth, 2, CMP_BLOCK // CMP_STRIDE, CMP_STRIDE, 1, HEAD_DIM)
    pos_r = jnp.broadcast_to(pos, pos.shape[:4] + (N_GROUPS, HEAD_DIM)).reshape(
        depth, 2, CMP_BLOCK // CMP_STRIDE, CMP_STRIDE * LANES)
    w1 = w_cmp1.reshape(depth, 2, CMP_BLOCK // CMP_STRIDE, CMP_STRIDE, HEAD_DIM, HEAD_DIM)
    w1_e = jnp.einsum("lkpjde,gh->lkpjgdhe", w1, eye).reshape(
        depth, 2, CMP_BLOCK // CMP_STRIDE, CMP_STRIDE * LANES, LANES).astype(BF16)
    w2_e = jnp.einsum("lkde,gh->lkgdhe", w_cmp2, eye).reshape(depth, 2, LANES, LANES).astype(BF16)
    return pos_r, w1_e, w2_e


def _selection_constants(seq):
    n_cmp = seq // CMP_STRIDE
    nb = seq // SEL_BLOCK
    est = (np.arange(seq)[:, None] // SEL_BLOCK == np.arange(LANES)[None, :]).astype(np.float32)
    cs = np.arange(n_cmp) * CMP_STRIDE
    js = np.arange(nb) * SEL_BLOCK
    ovt = ((cs[None, :] < js[:, None] + SEL_BLOCK) & (cs[None, :] + CMP_BLOCK > js[:, None])).astype(np.float32)
    return jnp.asarray(est, BF16), jnp.asarray(ovt, BF16)


def kernel(x, c, w_ada, b_ada, g_norm, w_ff_in, w_ff_out, w_mix_in, cmp_pos, w_cmp1, w_cmp2,
           conv_w, conv_b, g_mix_out, w_mix_out, g_final):
    bsz, seq, d = x.shape
    depth = w_ada.shape[0]
    assert seq % TOKEN_TILE == 0 and seq % Q_TILE == 0 and seq // CMP_STRIDE == LANES
    assert w_mix_out.shape[1] == 2 * ATT_WIDTH == d

    perm = _head_perm()
    w_ff_in_b = w_ff_in.astype(BF16)
    w_ff_out_b = w_ff_out.astype(BF16)
    w_in_p = _prep_mixer_in(w_mix_in)
    row_perm = np.concatenate([perm, np.arange(ATT_WIDTH, d)])
    w_out_p = w_mix_out[:, row_perm, :].astype(BF16)
    g_out_p = g_mix_out[:, row_perm]
    pos_r, w1_e, w2_e = _prep_compress(cmp_pos, w_cmp1, w_cmp2)
    est, ovt = _selection_constants(seq)

    mod_all = _ada_call(c, w_ada, b_ada).reshape(depth, bsz, 3 * N_SUB, d)
    gf = g_final.reshape(1, d)

    xt = x.reshape(bsz * seq, d)
    for l in range(depth):
        mod = mod_all[l]
        xt = _ffn_call(xt, mod, g_norm[l, 0:1], w_ff_in_b[l, 0], w_ff_out_b[l, 0], gf,
                       sub=0, seq=seq, final=False)
        q, kc, vc, kv, cv, gt = _inproj_call(xt, mod, g_norm[l, 1:2], w_in_p[l], seq=seq)
        oa = _attn_call(q, kc.reshape(bsz, seq, KV_COLS), vc.reshape(bsz, seq, KV_COLS),
                        kv.reshape(bsz, seq, 4 * KV_COLS), gt, pos_r[l], w1_e[l], w2_e[l], est, ovt,
                        bsz=bsz, seq=seq)
        xt = _outproj_call(oa, cv, xt, mod, conv_w[l], conv_b[l:l + 1], g_out_p[l:l + 1], w_out_p[l],
                           seq=seq)
        xt = _ffn_call(xt, mod, g_norm[l, 2:3], w_ff_in_b[l, 1], w_ff_out_b[l, 1], gf,
                       sub=2, seq=seq, final=(l == depth - 1))
    return xt.reshape(bsz, seq, d)
```

```python
import functools
import math

import numpy as np
import jax
import jax.numpy as jnp
from jax import lax
from jax.experimental import pallas as pl
from jax.experimental.pallas import tpu as pltpu

F32 = jnp.float32
BF16 = jnp.bfloat16

HEAD_DIM = 64
N_HEADS = 8
N_GROUPS = 2
HEADS_PER_GROUP = N_HEADS // N_GROUPS
ATT_WIDTH = N_HEADS * HEAD_DIM
KV_COLS = N_GROUPS * HEAD_DIM
CONV_K = 3
CMP_BLOCK = 32
CMP_STRIDE = 16
SEL_BLOCK = 64
N_SEL = 8
WINDOW = 512
N_SUB = 3
EPS = 1e-6
NEG_INF = -1e30
FORCE_BONUS = 1e4
LOG2_E = math.log2(math.e)

LANES = 128
V7X_VMEM_LIMIT = 56 * 1024 * 1024

TOKEN_TILE = 512
Q_TILE = 256
KEY_CHUNK = 256
FF_CHUNKS = ((0, 1536), (1536, 2816))


def _silu(v):
    return v * jax.nn.sigmoid(v)


def _rms(v, g):
    ms = jnp.mean(v * v, axis=-1, keepdims=True)
    return (v * lax.rsqrt(ms + EPS)) * g


def _prenorm(x, g, mod_ref, sub):
    shift = mod_ref[3 * sub:3 * sub + 1, :]
    scale = mod_ref[3 * sub + 1:3 * sub + 2, :]
    return _rms(x, g) * (1.0 + scale) + shift


def _nt_dot(a, b):
    return lax.dot_general(a, b, (((1,), (1,)), ((), ())), preferred_element_type=F32)


def _ada_kernel(c_ref, w_ref, b_ref, o_ref):
    ca = _silu(c_ref[...]).astype(BF16)
    o_ref[...] = jnp.dot(ca, w_ref[...].astype(BF16), preferred_element_type=F32) + b_ref[...]


def _ada_call(c, w_ada, b_ada):
    depth, d, n = w_ada.shape
    bsz = c.shape[0]
    tn = n // 4
    return pl.pallas_call(
        _ada_kernel,
        grid=(depth, n // tn),
        in_specs=[
            pl.BlockSpec((bsz, d), lambda l, j: (0, 0)),
            pl.BlockSpec((None, d, tn), lambda l, j: (l, 0, j)),
            pl.BlockSpec((None, 1, tn), lambda l, j: (l, 0, j)),
        ],
        out_specs=pl.BlockSpec((None, bsz, tn), lambda l, j: (l, 0, j)),
        out_shape=jax.ShapeDtypeStruct((depth, bsz, n), F32),
        compiler_params=pltpu.CompilerParams(
            dimension_semantics=("arbitrary", "arbitrary"), vmem_limit_bytes=V7X_VMEM_LIMIT),
        name="adaln_mod",
    )(c, w_ada, b_ada.reshape(depth, 1, n))


def _ffn_kernel(x_ref, mod_ref, g_ref, win_ref, wout_ref, gf_ref, o_ref, *, sub, d_ff, final):
    x = x_ref[...]
    hb = _prenorm(x, g_ref[...], mod_ref, sub).astype(BF16)
    y = None
    for c0, c1 in FF_CHUNKS:
        gate = jnp.dot(hb, win_ref[:, c0:c1], preferred_element_type=F32)
        up = jnp.dot(hb, win_ref[:, d_ff + c0:d_ff + c1], preferred_element_type=F32)
        act = (_silu(gate) * up).astype(BF16)
        part = jnp.dot(act, wout_ref[c0:c1, :], preferred_element_type=F32)
        y = part if y is None else y + part
    out = x + (0.5 * mod_ref[3 * sub + 2:3 * sub + 3, :]) * y
    if final:
        out = _rms(out, gf_ref[...])
    o_ref[...] = out


def _ffn_call(x, mod, g, w_in, w_out, g_final, *, sub, seq, final):
    t, d = x.shape
    d_ff = w_out.shape[0]
    tiles_per_seq = seq // TOKEN_TILE
    const = lambda i: (0, 0)
    return pl.pallas_call(
        functools.partial(_ffn_kernel, sub=sub, d_ff=d_ff, final=final),
        grid=(t // TOKEN_TILE,),
        in_specs=[
            pl.BlockSpec((TOKEN_TILE, d), lambda i: (i, 0)),
            pl.BlockSpec((None, 3 * N_SUB, d), lambda i: (i // tiles_per_seq, 0, 0)),
            pl.BlockSpec((1, d), const),
            pl.BlockSpec((d, 2 * d_ff), const, pipeline_mode=pl.Buffered(1)),
            pl.BlockSpec((d_ff, d), const, pipeline_mode=pl.Buffered(1)),
            pl.BlockSpec((1, d), const),
        ],
        out_specs=pl.BlockSpec((TOKEN_TILE, d), lambda i: (i, 0)),
        out_shape=jax.ShapeDtypeStruct((t, d), F32),
        compiler_params=pltpu.CompilerParams(
            dimension_semantics=("arbitrary",), vmem_limit_bytes=V7X_VMEM_LIMIT),
        name="ffn_final" if final else "ffn",
    )(x, mod, g, w_in, w_out, g_final)


_Q0, _KC0, _VC0, _KV0, _CV0, _GT0, _IN_P = 0, 512, 640, 768, 1280, 2816, 2944
_KS, _VS, _KW, _VW = 0, LANES, 2 * LANES, 3 * LANES


def _inproj_kernel(x_ref, mod_ref, g_ref, w_ref, q_ref, kc_ref, vc_ref, kv_ref, cv_ref, gt_ref):
    hb = _prenorm(x_ref[...], g_ref[...], mod_ref, 1).astype(BF16)

    def proj(c0, c1):
        return jnp.dot(hb, w_ref[:, c0:c1], preferred_element_type=F32)

    q_ref[...] = proj(_Q0, _KC0).astype(BF16)
    kc_ref[...] = proj(_KC0, _VC0)
    vc_ref[...] = proj(_VC0, _KV0)
    kv_ref[...] = proj(_KV0, _CV0).astype(BF16)
    cv_ref[...] = proj(_CV0, _GT0)
    gt_ref[...] = proj(_GT0, _IN_P)


def _inproj_call(x, mod, g, w_in_p, *, seq):
    t, d = x.shape
    tiles_per_seq = seq // TOKEN_TILE
    const = lambda i: (0, 0)
    row = lambda i: (i, 0)
    widths = (_KC0 - _Q0, KV_COLS, KV_COLS, _CV0 - _KV0, _GT0 - _CV0, _IN_P - _GT0)
    dtypes = (BF16, F32, F32, BF16, F32, F32)
    return pl.pallas_call(
        _inproj_kernel,
        grid=(t // TOKEN_TILE,),
        in_specs=[
            pl.BlockSpec((TOKEN_TILE, d), row),
            pl.BlockSpec((None, 3 * N_SUB, d), lambda i: (i // tiles_per_seq, 0, 0)),
            pl.BlockSpec((1, d), const),
            pl.BlockSpec((d, _IN_P), const, pipeline_mode=pl.Buffered(1)),
        ],
        out_specs=[pl.BlockSpec((TOKEN_TILE, w), row) for w in widths],
        out_shape=[jax.ShapeDtypeStruct((t, w), dt) for w, dt in zip(widths, dtypes)],
        compiler_params=pltpu.CompilerParams(
            dimension_semantics=("arbitrary",), vmem_limit_bytes=V7X_VMEM_LIMIT),
        name="mixer_inproj",
    )(x, mod, g, w_in_p)


def _attn_kernel(q_ref, kc_ref, vc_ref, kv_ref, gt_ref, pos_ref, w1_ref, w2_ref, est_ref, ovt_ref, gx_ref,
                 o_ref, kcmp_ref, vcmp_ref, s_ref, mx_ref, acc_ref, *, seq):
    j = pl.program_id(1)
    tq = Q_TILE
    ck = KEY_CHUNK
    rows = N_HEADS * tq
    n_cmp = seq // CMP_STRIDE
    per_row = CMP_BLOCK // CMP_STRIDE
    nb = seq // SEL_BLOCK

    @pl.when(j == 0)
    def _():
        for kvi, (src, dst) in enumerate(((kc_ref, kcmp_ref), (vc_ref, vcmp_ref))):
            r = jnp.concatenate(
                [src[pl.ds(jj, n_cmp, stride=CMP_STRIDE), :] for jj in range(CMP_STRIDE)], axis=1)
            pre = None
            for part in range(per_row):
                xin = (r + pos_ref[kvi, part:part + 1, :]).astype(BF16)
                prod = jnp.dot(xin, w1_ref[kvi, part], preferred_element_type=F32)
                if part:
                    prod = pltpu.roll(prod, n_cmp - part, 0)
                pre = prod if pre is None else pre + prod
            act = _silu(pre).astype(BF16)
            dst[...] = jnp.dot(act, w2_ref[kvi], preferred_element_type=F32).astype(BF16)

    t0 = j * tq
    lane_q = lax.broadcasted_iota(jnp.int32, (tq, LANES), 1)
    q_loc = lax.broadcasted_iota(jnp.int32, (rows, ck), 0) & (tq - 1)
    key_loc = lax.broadcasted_iota(jnp.int32, (rows, ck), 1)
    causal = key_loc <= q_loc
    win_tail = key_loc > q_loc
    q_loc_c = lax.broadcasted_iota(jnp.int32, (rows, n_cmp), 0) & (tq - 1)
    cmask = (lax.broadcasted_iota(jnp.int32, (rows, n_cmp), 1) * CMP_STRIDE + (CMP_BLOCK - 1)) <= t0 + q_loc_c
    sg = jax.nn.sigmoid(gt_ref[...])
    sg_hi = sg.astype(BF16)
    sg_lo = (sg - sg_hi.astype(F32)).astype(BF16)
    gates = jnp.dot(jnp.concatenate([sg_hi, sg_lo], axis=1), gx_ref[...],
                    preferred_element_type=F32)

    jb = lax.broadcasted_iota(jnp.int32, (nb, tq), 0)
    cur = (t0 + lax.broadcasted_iota(jnp.int32, (nb, tq), 1)) // SEL_BLOCK
    valid = jb <= cur
    forced = ((jb == 0) | (jb == cur) | (jb == cur - 1)).astype(F32)

    def kv_rows(col, c0, n):
        return kv_ref[pl.ds(pl.multiple_of(c0 * ck, ck), n * ck), col:col + LANES]

    def sel_keys(c0, n):
        est = est_ref[pl.ds(pl.multiple_of(c0 * ck, ck), n * ck), :]
        return jnp.concatenate([kv_rows(_KS, c0, n), est], axis=1)

    def with_ones(v):
        return jnp.concatenate([v, jnp.ones(v.shape, BF16)], axis=1)

    def store_scores(s, c0, n):
        smax = None
        for k in range(n):
            s_ref[c0 + k] = s[:, k * ck:(k + 1) * ck]
            for h in range(ck // LANES):
                part = s[:, k * ck + h * LANES:k * ck + (h + 1) * LANES]
                smax = part if smax is None else jnp.maximum(smax, part)
        mx_ref[...] = jnp.maximum(mx_ref[...], smax)

    def probs(c0, n, m):
        s = s_ref[c0] if n == 1 else jnp.concatenate([s_ref[c0 + k] for k in range(n)], axis=1)
        return jnp.exp2(s - m).astype(BF16)

    def normalised(ov):
        return ov[:, :LANES] / ov[:, LANES:]

    n_pairs = lax.shift_right_logical(j, 1)
    odd = (j & 1) == 1
    win_a = jnp.maximum(j - 2, 0)
    win_b = jnp.maximum(j - 1, 0)
    off_a = jnp.where(j >= 2, 0.0, NEG_INF)
    off_b = jnp.where(j >= 1, 0.0, NEG_INF)

    qs = []
    for g in range(N_GROUPS):
        keep = (lane_q < HEAD_DIM) if g == 0 else (lane_q >= HEAD_DIM)
        for i in range(HEADS_PER_GROUP):
            qi = q_ref[:, i * LANES:(i + 1) * LANES]
            qs.append(jnp.where(keep, qi, jnp.zeros_like(qi)))
    qst = jnp.concatenate(qs, axis=0)

    sc = jnp.where(cmask, _nt_dot(qst, kcmp_ref[...]), NEG_INF)
    e = jnp.exp2(sc - jnp.max(sc, axis=1, keepdims=True))
    p = e / jnp.sum(e, axis=1, keepdims=True)
    pb = jnp.where(cmask, p, 0.0).astype(BF16)
    o_cmp = jnp.dot(pb, vcmp_ref[...], preferred_element_type=F32)

    bias_rows = []
    for g in range(N_GROUPS):
        imp = None
        for i in range(HEADS_PER_GROUP):
            r0 = (g * HEADS_PER_GROUP + i) * tq
            part = _nt_dot(ovt_ref[...], pb[r0:r0 + tq, :])
            imp = part if imp is None else imp + part
        val = jnp.where(valid, imp + FORCE_BONUS * forced, NEG_INF)
        rank = jnp.zeros((nb, tq), F32)
        for k in range(nb):
            vk = val[k:k + 1, :]
            tie = jnp.where(jb > k, 1.0, 0.0)
            rank = rank + jnp.where(vk > val, 1.0, jnp.where(vk == val, tie, 0.0))
        chosen = jnp.where(valid, jnp.where(rank < N_SEL, 0.0, NEG_INF), NEG_INF)
        bias_t = jnp.concatenate([chosen, jnp.zeros((LANES - nb, tq), F32)], axis=0)
        bias_rows.extend([bias_t.T.astype(BF16)] * HEADS_PER_GROUP)
    qaug = jnp.concatenate([qst, jnp.concatenate(bias_rows, axis=0)], axis=1)

    kw = jnp.concatenate([kv_rows(_KW, win_a, 1), kv_rows(_KW, win_b, 1), kv_rows(_KW, j, 1)], axis=0)
    vw = jnp.concatenate([kv_rows(_VW, win_a, 1), kv_rows(_VW, win_b, 1), kv_rows(_VW, j, 1)], axis=0)
    sw = _nt_dot(qst, kw)
    sw = jnp.concatenate([jnp.where(win_tail, sw[:, :ck] + off_a, NEG_INF),
                          sw[:, ck:2 * ck] + off_b,
                          jnp.where(causal, sw[:, 2 * ck:], NEG_INF)], axis=1)
    pw = jnp.exp2(sw - jnp.max(sw, axis=1, keepdims=True)).astype(BF16)

    s_d = jnp.where(causal, _nt_dot(qaug, sel_keys(j, 1)), NEG_INF)
    s_ref[j] = s_d
    mx_ref[...] = jnp.maximum(s_d[:, :LANES], s_d[:, LANES:])

    def pair_score(u, carry):
        store_scores(_nt_dot(qaug, sel_keys(2 * u, 2)), 2 * u, 2)
        return carry

    lax.fori_loop(0, n_pairs, pair_score, 0)

    @pl.when(odd)
    def _():
        store_scores(_nt_dot(qaug, sel_keys(j - 1, 1)), j - 1, 1)

    m_sel = jnp.max(mx_ref[...], axis=1, keepdims=True)
    acc_ref[...] = jnp.dot(probs(j, 1, m_sel), with_ones(kv_rows(_VS, j, 1)),
                           preferred_element_type=F32)

    vw1 = with_ones(vw)
    half = rows // 2
    o_win = normalised(jnp.concatenate(
        [jnp.dot(pw[:half], vw1, preferred_element_type=F32),
         jnp.dot(pw[half:], vw1, preferred_element_type=F32)], axis=0))

    def gate(br, g, i):
        cix = br * N_HEADS + g * HEADS_PER_GROUP + i
        return gates[:, cix * LANES:(cix + 1) * LANES]

    head_rows = [((g * HEADS_PER_GROUP + i) * tq, g, i)
                 for i in range(HEADS_PER_GROUP) for g in range(N_GROUPS)]
    t_cmp = [gate(0, g, i) * o_cmp[r0:r0 + tq, :] for r0, g, i in head_rows]
    t_win = [gate(2, g, i) * o_win[r0:r0 + tq, :] for r0, g, i in head_rows]

    def pair_value(u, carry):
        acc_ref[...] = acc_ref[...] + jnp.dot(
            probs(2 * u, 2, m_sel), with_ones(kv_rows(_VS, 2 * u, 2)), preferred_element_type=F32)
        return carry

    lax.fori_loop(0, n_pairs, pair_value, 0)

    @pl.when(odd)
    def _():
        acc_ref[...] = acc_ref[...] + jnp.dot(
            probs(j - 1, 1, m_sel), with_ones(kv_rows(_VS, j - 1, 1)), preferred_element_type=F32)

    o_sel = normalised(acc_ref[...])

    tots = [(t_cmp[n] + gate(1, g, i) * o_sel[r0:r0 + tq, :]) + t_win[n]
            for n, (r0, g, i) in enumerate(head_rows)]
    for i in range(HEADS_PER_GROUP):
        o_ref[:, i * LANES:(i + 1) * LANES] = jnp.where(
            lane_q < HEAD_DIM, tots[N_GROUPS * i], tots[N_GROUPS * i + 1])


def _attn_call(q, kc, vc, kv, gt, pos_r, w1_e, w2_e, est, ovt, gx, *, bsz, seq):
    assert WINDOW == 2 * KEY_CHUNK and Q_TILE == KEY_CHUNK
    nq = seq // Q_TILE
    rows = N_HEADS * Q_TILE
    n_cmp = seq // CMP_STRIDE
    tile = lambda b, j: (b * nq + j, 0)
    per_seq = lambda b, j: (b, 0, 0)
    c2 = lambda b, j: (0, 0)
    c3 = lambda b, j: (0, 0, 0)
    c4 = lambda b, j: (0, 0, 0, 0)
    return pl.pallas_call(
        functools.partial(_attn_kernel, seq=seq),
        grid=(bsz, nq),
        in_specs=[
            pl.BlockSpec((Q_TILE, ATT_WIDTH), tile),
            pl.BlockSpec((None, seq, KV_COLS), per_seq),
            pl.BlockSpec((None, seq, KV_COLS), per_seq),
            pl.BlockSpec((None, seq, 4 * KV_COLS), per_seq),
            pl.BlockSpec((Q_TILE, LANES), tile),
            pl.BlockSpec(pos_r.shape, c3),
            pl.BlockSpec(w1_e.shape, c4),
            pl.BlockSpec(w2_e.shape, c3),
            pl.BlockSpec(est.shape, c2),
            pl.BlockSpec(ovt.shape, c2),
            pl.BlockSpec(gx.shape, c2),
        ],
        out_specs=pl.BlockSpec((Q_TILE, ATT_WIDTH), tile),
        out_shape=jax.ShapeDtypeStruct((bsz * seq, ATT_WIDTH), F32),
        scratch_shapes=[
            pltpu.VMEM((n_cmp, LANES), BF16),
            pltpu.VMEM((n_cmp, LANES), BF16),
            pltpu.VMEM((seq // KEY_CHUNK, rows, KEY_CHUNK), F32),
            pltpu.VMEM((rows, LANES), F32),
            pltpu.VMEM((rows, 2 * LANES), F32),
        ],
        compiler_params=pltpu.CompilerParams(
            dimension_semantics=("arbitrary", "arbitrary"), vmem_limit_bytes=V7X_VMEM_LIMIT),
        name="nsa_attention",
    )(q, kc, vc, kv, gt, pos_r, w1_e, w2_e, est, ovt, gx)


def _outproj_kernel(oa_ref, cv_ref, halo_ref, x_ref, mod_ref, cw_ref, cb_ref, go_ref, w_ref, o_ref,
                    *, tiles_per_seq):
    i = pl.program_id(0)
    cw = ATT_WIDTH
    hc = cv_ref[:, 0:cw]
    bg = cv_ref[:, cw:2 * cw]
    cg = cv_ref[:, 2 * cw:3 * cw]
    u = cg * hc
    hu = halo_ref[:, 2 * cw:3 * cw] * halo_ref[:, 0:cw]
    hu = jnp.where(i % tiles_per_seq == 0, 0.0, hu)
    row = lax.broadcasted_iota(jnp.int32, u.shape, 0)
    u1 = jnp.where(row == 0, hu[7:8, :], pltpu.roll(u, 1, 0))
    u2 = jnp.where(row == 0, hu[6:7, :], jnp.where(row == 1, hu[7:8, :], pltpu.roll(u, 2, 0)))
    v = cb_ref[...] + u2 * cw_ref[0:1, :]
    v = v + u1 * cw_ref[1:2, :]
    v = v + u * cw_ref[2:3, :]
    oc = bg * v
    oa = _rms(oa_ref[...], go_ref[:, 0:cw])
    oc = _rms(oc, go_ref[:, cw:2 * cw])
    ob = jnp.concatenate([oa, oc], axis=1).astype(BF16)
    y = jnp.dot(ob, w_ref[...], preferred_element_type=F32)
    o_ref[...] = x_ref[...] + mod_ref[5:6, :] * y


def _outproj_call(oa, cv, x, mod, conv_w, conv_b, g_out_p, w_out_p, *, seq):
    t, d = x.shape
    tiles_per_seq = seq // TOKEN_TILE
    halo_blocks = TOKEN_TILE // 8
    const = lambda i: (0, 0)
    row = lambda i: (i, 0)
    return pl.pallas_call(
        functools.partial(_outproj_kernel, tiles_per_seq=tiles_per_seq),
        grid=(t // TOKEN_TILE,),
        in_specs=[
            pl.BlockSpec((TOKEN_TILE, oa.shape[1]), row),
            pl.BlockSpec((TOKEN_TILE, cv.shape[1]), row),
            pl.BlockSpec((8, cv.shape[1]), lambda i: (jnp.maximum(i * halo_blocks - 1, 0), 0)),
            pl.BlockSpec((TOKEN_TILE, d), row),
            pl.BlockSpec((None, 3 * N_SUB, d), lambda i: (i // tiles_per_seq, 0, 0)),
            pl.BlockSpec(conv_w.shape, const),
            pl.BlockSpec(conv_b.shape, const),
            pl.BlockSpec(g_out_p.shape, const),
            pl.BlockSpec(w_out_p.shape, const, pipeline_mode=pl.Buffered(1)),
        ],
        out_specs=pl.BlockSpec((TOKEN_TILE, d), row),
        out_shape=jax.ShapeDtypeStruct((t, d), F32),
        compiler_params=pltpu.CompilerParams(
            dimension_semantics=("arbitrary",), vmem_limit_bytes=V7X_VMEM_LIMIT),
        name="mixer_outproj",
    )(oa, cv, cv, x, mod, conv_w, conv_b, g_out_p, w_out_p)


def _head_perm():
    idx = np.empty(ATT_WIDTH, np.int32)
    for i in range(HEADS_PER_GROUP):
        for g in range(N_GROUPS):
            h = g * HEADS_PER_GROUP + i
            dst = i * LANES + g * HEAD_DIM
            idx[dst:dst + HEAD_DIM] = np.arange(h * HEAD_DIM, (h + 1) * HEAD_DIM)
    return idx


def _gate_perm():
    return np.array([h * 3 + br for br in range(3) for h in range(N_HEADS)], np.int32)


def _prep_mixer_in(w):
    offs = np.cumsum([0, ATT_WIDTH] + [KV_COLS] * 6 + [3 * N_HEADS] + [ATT_WIDTH] * 3)
    q = w[..., offs[0]:offs[1]][..., _head_perm()] * (LOG2_E * HEAD_DIM ** -0.5)
    kvs = w[..., offs[1]:offs[7]]
    gates = w[..., offs[7]:offs[8]][..., _gate_perm()]
    conv = w[..., offs[8]:offs[11]]
    pad = jnp.zeros(w.shape[:-1] + (LANES - 3 * N_HEADS,), w.dtype)
    return jnp.concatenate([q, kvs, conv, gates, pad], axis=-1).astype(BF16)


def _prep_compress(cmp_pos, w_cmp1, w_cmp2):
    depth = cmp_pos.shape[0]
    eye = jnp.eye(N_GROUPS, dtype=F32)
    pos = cmp_pos.reshape(depth, 2, CMP_BLOCK // CMP_STRIDE, CMP_STRIDE, 1, HEAD_DIM)
    pos_r = jnp.broadcast_to(pos, pos.shape[:4] + (N_GROUPS, HEAD_DIM)).reshape(
        depth, 2, CMP_BLOCK // CMP_STRIDE, CMP_STRIDE * LANES)
    w1 = w_cmp1.reshape(depth, 2, CMP_BLOCK // CMP_STRIDE, CMP_STRIDE, HEAD_DIM, HEAD_DIM)
    w1_e = jnp.einsum("lkpjde,gh->lkpjgdhe", w1, eye).reshape(
        depth, 2, CMP_BLOCK // CMP_STRIDE, CMP_STRIDE * LANES, LANES).astype(BF16)
    w2_e = jnp.einsum("lkde,gh->lkgdhe", w_cmp2, eye).reshape(depth, 2, LANES, LANES).astype(BF16)
    return pos_r, w1_e, w2_e


def _selection_constants(seq):
    n_cmp = seq // CMP_STRIDE
    nb = seq // SEL_BLOCK
    est = (np.arange(seq)[:, None] // SEL_BLOCK == np.arange(LANES)[None, :]).astype(np.float32)
    cs = np.arange(n_cmp) * CMP_STRIDE
    js = np.arange(nb) * SEL_BLOCK
    ovt = ((cs[None, :] < js[:, None] + SEL_BLOCK) & (cs[None, :] + CMP_BLOCK > js[:, None])).astype(np.float32)
    n_gates = 3 * N_HEADS
    gx_half = (np.arange(LANES)[:, None] == np.arange(n_gates * LANES)[None, :] // LANES).astype(np.float32)
    gx = np.concatenate([gx_half, gx_half], axis=0)
    return jnp.asarray(est, BF16), jnp.asarray(ovt, BF16), jnp.asarray(gx, BF16)


def kernel(x, c, w_ada, b_ada, g_norm, w_ff_in, w_ff_out, w_mix_in, cmp_pos, w_cmp1, w_cmp2,
           conv_w, conv_b, g_mix_out, w_mix_out, g_final):
    bsz, seq, d = x.shape
    depth = w_ada.shape[0]
    assert seq % TOKEN_TILE == 0 and seq % Q_TILE == 0 and seq // CMP_STRIDE == LANES
    assert w_mix_out.shape[1] == 2 * ATT_WIDTH == d

    perm = _head_perm()
    w_ff_in_b = w_ff_in.astype(BF16)
    w_ff_out_b = w_ff_out.astype(BF16)
    w_in_p = _prep_mixer_in(w_mix_in)
    row_perm = np.concatenate([perm, np.arange(ATT_WIDTH, d)])
    w_out_p = w_mix_out[:, row_perm, :].astype(BF16)
    g_out_p = g_mix_out[:, row_perm]
    pos_r, w1_e, w2_e = _prep_compress(cmp_pos, w_cmp1, w_cmp2)
    est, ovt, gx = _selection_constants(seq)

    mod_all = _ada_call(c, w_ada, b_ada).reshape(depth, bsz, 3 * N_SUB, d)
    gf = g_final.reshape(1, d)

    xt = x.reshape(bsz * seq, d)
    for l in range(depth):
        mod = mod_all[l]
        xt = _ffn_call(xt, mod, g_norm[l, 0:1], w_ff_in_b[l, 0], w_ff_out_b[l, 0], gf,
                       sub=0, seq=seq, final=False)
        q, kc, vc, kv, cv, gt = _inproj_call(xt, mod, g_norm[l, 1:2], w_in_p[l], seq=seq)
        oa = _attn_call(q, kc.reshape(bsz, seq, KV_COLS), vc.reshape(bsz, seq, KV_COLS),
                        kv.reshape(bsz, seq, 4 * KV_COLS), gt, pos_r[l], w1_e[l], w2_e[l], est, ovt, gx,
                        bsz=bsz, seq=seq)
        xt = _outproj_call(oa, cv, xt, mod, conv_w[l], conv_b[l:l + 1], g_out_p[l:l + 1], w_out_p[l],
                           seq=seq)
        xt = _ffn_call(xt, mod, g_norm[l, 2:3], w_ff_in_b[l, 1], w_ff_out_b[l, 1], gf,
                       sub=2, seq=seq, final=(l == depth - 1))
    return xt.reshape(bsz, seq, d)
```

```python
import functools
import math

import numpy as np
import jax
import jax.numpy as jnp
from jax import lax
from jax.experimental import pallas as pl
from jax.experimental.pallas import tpu as pltpu

F32 = jnp.float32
BF16 = jnp.bfloat16

HEAD_DIM = 64
N_HEADS = 8
N_GROUPS = 2
HEADS_PER_GROUP = N_HEADS // N_GROUPS
ATT_WIDTH = N_HEADS * HEAD_DIM
KV_COLS = N_GROUPS * HEAD_DIM
CONV_K = 3
CMP_BLOCK = 32
CMP_STRIDE = 16
SEL_BLOCK = 64
N_SEL = 8
WINDOW = 512
N_SUB = 3
EPS = 1e-6
NEG_INF = -1e30
FORCE_BONUS = 1e4
LOG2_E = math.log2(math.e)

LANES = 128
V7X_VMEM_LIMIT = 56 * 1024 * 1024

TOKEN_TILE = 512
Q_TILE = 256
KEY_CHUNK = 256
FF_CHUNKS = ((0, 1536), (1536, 2816))


def _silu(v):
    return v * jax.nn.sigmoid(v)


def _rms(v, g):
    ms = jnp.mean(v * v, axis=-1, keepdims=True)
    return (v * lax.rsqrt(ms + EPS)) * g


def _prenorm(x, g, mod_ref, sub):
    shift = mod_ref[3 * sub:3 * sub + 1, :]
    scale = mod_ref[3 * sub + 1:3 * sub + 2, :]
    return _rms(x, g) * (1.0 + scale) + shift


def _nt_dot(a, b):
    return lax.dot_general(a, b, (((1,), (1,)), ((), ())), preferred_element_type=F32)


def _ada_kernel(c_ref, w_ref, b_ref, o_ref):
    ca = _silu(c_ref[...]).astype(BF16)
    o_ref[...] = jnp.dot(ca, w_ref[...].astype(BF16), preferred_element_type=F32) + b_ref[...]


def _ada_call(c, w_ada, b_ada):
    depth, d, n = w_ada.shape
    bsz = c.shape[0]
    tn = n // 4
    return pl.pallas_call(
        _ada_kernel,
        grid=(depth, n // tn),
        in_specs=[
            pl.BlockSpec((bsz, d), lambda l, j: (0, 0)),
            pl.BlockSpec((None, d, tn), lambda l, j: (l, 0, j)),
            pl.BlockSpec((None, 1, tn), lambda l, j: (l, 0, j)),
        ],
        out_specs=pl.BlockSpec((None, bsz, tn), lambda l, j: (l, 0, j)),
        out_shape=jax.ShapeDtypeStruct((depth, bsz, n), F32),
        compiler_params=pltpu.CompilerParams(
            dimension_semantics=("arbitrary", "arbitrary"), vmem_limit_bytes=V7X_VMEM_LIMIT),
        name="adaln_mod",
    )(c, w_ada, b_ada.reshape(depth, 1, n))


def _ffn(x, mod_ref, g, win_ref, wout_ref, sub):
    d_ff = wout_ref.shape[0]
    hb = _prenorm(x, g, mod_ref, sub).astype(BF16)
    y = None
    for c0, c1 in FF_CHUNKS:
        gate = jnp.dot(hb, win_ref[:, c0:c1], preferred_element_type=F32)
        up = jnp.dot(hb, win_ref[:, d_ff + c0:d_ff + c1], preferred_element_type=F32)
        act = (_silu(gate) * up).astype(BF16)
        part = jnp.dot(act, wout_ref[c0:c1, :], preferred_element_type=F32)
        y = part if y is None else y + part
    return x + (0.5 * mod_ref[3 * sub + 2:3 * sub + 3, :]) * y


_Q0, _KC0, _VC0, _KV0, _CV0, _GT0, _IN_P = 0, 512, 640, 768, 1280, 2816, 2944
_KS, _VS, _KW, _VW = 0, LANES, 2 * LANES, 3 * LANES


def _pre_kernel(x_ref, mod_ref, g_ref, win_ref, wout_ref, wmix_ref,
                x1_ref, q_ref, kc_ref, vc_ref, kv_ref, cv_ref, gt_ref):
    x1 = _ffn(x_ref[...], mod_ref, g_ref[0:1, :], win_ref, wout_ref, 0)
    x1_ref[...] = x1
    hb = _prenorm(x1, g_ref[1:2, :], mod_ref, 1).astype(BF16)

    def proj(c0, c1):
        return jnp.dot(hb, wmix_ref[:, c0:c1], preferred_element_type=F32)

    q_ref[...] = proj(_Q0, _KC0).astype(BF16)
    kc_ref[...] = proj(_KC0, _VC0)
    vc_ref[...] = proj(_VC0, _KV0)
    kv_ref[...] = proj(_KV0, _CV0).astype(BF16)
    cv_ref[...] = proj(_CV0, _GT0)
    gt_ref[...] = proj(_GT0, _IN_P)


def _pre_call(x, mod, g, w_in, w_out, w_mix_p, *, seq):
    t, d = x.shape
    d_ff = w_out.shape[0]
    tiles_per_seq = seq // TOKEN_TILE
    const = lambda i: (0, 0)
    row = lambda i: (i, 0)
    widths = (d, _KC0 - _Q0, KV_COLS, KV_COLS, _CV0 - _KV0, _GT0 - _CV0, _IN_P - _GT0)
    dtypes = (F32, BF16, F32, F32, BF16, F32, F32)
    return pl.pallas_call(
        _pre_kernel,
        grid=(t // TOKEN_TILE,),
        in_specs=[
            pl.BlockSpec((TOKEN_TILE, d), row),
            pl.BlockSpec((None, 3 * N_SUB, d), lambda i: (i // tiles_per_seq, 0, 0)),
            pl.BlockSpec((N_SUB, d), const),
            pl.BlockSpec((d, 2 * d_ff), const, pipeline_mode=pl.Buffered(1)),
            pl.BlockSpec((d_ff, d), const, pipeline_mode=pl.Buffered(1)),
            pl.BlockSpec((d, _IN_P), const, pipeline_mode=pl.Buffered(1)),
        ],
        out_specs=[pl.BlockSpec((TOKEN_TILE, w), row) for w in widths],
        out_shape=[jax.ShapeDtypeStruct((t, w), dt) for w, dt in zip(widths, dtypes)],
        compiler_params=pltpu.CompilerParams(
            dimension_semantics=("arbitrary",), vmem_limit_bytes=V7X_VMEM_LIMIT),
        name="ffn_inproj",
    )(x, mod, g, w_in, w_out, w_mix_p)


def _attn_kernel(q_ref, kc_ref, vc_ref, kv_ref, gt_ref, pos_ref, w1_ref, w2_ref, est_ref, ovt_ref, gx_ref,
                 o_ref, kcmp_ref, vcmp_ref, s_ref, mx_ref, acc_ref, *, seq):
    j = pl.program_id(1)
    tq = Q_TILE
    ck = KEY_CHUNK
    rows = N_HEADS * tq
    n_cmp = seq // CMP_STRIDE
    per_row = CMP_BLOCK // CMP_STRIDE
    nb = seq // SEL_BLOCK

    @pl.when(j == 0)
    def _():
        for kvi, (src, dst) in enumerate(((kc_ref, kcmp_ref), (vc_ref, vcmp_ref))):
            r = jnp.concatenate(
                [src[pl.ds(jj, n_cmp, stride=CMP_STRIDE), :] for jj in range(CMP_STRIDE)], axis=1)
            pre = None
            for part in range(per_row):
                xin = (r + pos_ref[kvi, part:part + 1, :]).astype(BF16)
                prod = jnp.dot(xin, w1_ref[kvi, part], preferred_element_type=F32)
                if part:
                    prod = pltpu.roll(prod, n_cmp - part, 0)
                pre = prod if pre is None else pre + prod
            act = _silu(pre).astype(BF16)
            dst[...] = jnp.dot(act, w2_ref[kvi], preferred_element_type=F32).astype(BF16)

    t0 = j * tq
    lane_q = lax.broadcasted_iota(jnp.int32, (tq, LANES), 1)
    q_loc = lax.broadcasted_iota(jnp.int32, (rows, ck), 0) & (tq - 1)
    key_loc = lax.broadcasted_iota(jnp.int32, (rows, ck), 1)
    causal = key_loc <= q_loc
    win_tail = key_loc > q_loc
    q_loc_c = lax.broadcasted_iota(jnp.int32, (rows, n_cmp), 0) & (tq - 1)
    cmask = (lax.broadcasted_iota(jnp.int32, (rows, n_cmp), 1) * CMP_STRIDE + (CMP_BLOCK - 1)) <= t0 + q_loc_c
    sg = jax.nn.sigmoid(gt_ref[...])
    sg_hi = sg.astype(BF16)
    sg_lo = (sg - sg_hi.astype(F32)).astype(BF16)
    gates = jnp.dot(jnp.concatenate([sg_hi, sg_lo], axis=1), gx_ref[...],
                    preferred_element_type=F32)

    jb = lax.broadcasted_iota(jnp.int32, (nb, tq), 0)
    cur = (t0 + lax.broadcasted_iota(jnp.int32, (nb, tq), 1)) // SEL_BLOCK
    valid = jb <= cur
    forced = ((jb == 0) | (jb == cur) | (jb == cur - 1)).astype(F32)

    def kv_rows(col, c0, n):
        return kv_ref[pl.ds(pl.multiple_of(c0 * ck, ck), n * ck), col:col + LANES]

    def sel_keys(c0, n):
        est = est_ref[pl.ds(pl.multiple_of(c0 * ck, ck), n * ck), :]
        return jnp.concatenate([kv_rows(_KS, c0, n), est], axis=1)

    def with_ones(v):
        return jnp.concatenate([v, jnp.ones(v.shape, BF16)], axis=1)

    def store_scores(s, c0, n):
        smax = None
        for k in range(n):
            s_ref[c0 + k] = s[:, k * ck:(k + 1) * ck]
            for h in range(ck // LANES):
                part = s[:, k * ck + h * LANES:k * ck + (h + 1) * LANES]
                smax = part if smax is None else jnp.maximum(smax, part)
        mx_ref[...] = jnp.maximum(mx_ref[...], smax)

    def probs(c0, n, m):
        s = s_ref[c0] if n == 1 else jnp.concatenate([s_ref[c0 + k] for k in range(n)], axis=1)
        return jnp.exp2(s - m).astype(BF16)

    def normalised(ov):
        return ov[:, :LANES] / ov[:, LANES:]

    n_pairs = lax.shift_right_logical(j, 1)
    odd = (j & 1) == 1
    win_a = jnp.maximum(j - 2, 0)
    win_b = jnp.maximum(j - 1, 0)
    off_a = jnp.where(j >= 2, 0.0, NEG_INF)
    off_b = jnp.where(j >= 1, 0.0, NEG_INF)

    qs = []
    for g in range(N_GROUPS):
        keep = (lane_q < HEAD_DIM) if g == 0 else (lane_q >= HEAD_DIM)
        for i in range(HEADS_PER_GROUP):
            qi = q_ref[:, i * LANES:(i + 1) * LANES]
            qs.append(jnp.where(keep, qi, jnp.zeros_like(qi)))
    qst = jnp.concatenate(qs, axis=0)

    sc = jnp.where(cmask, _nt_dot(qst, kcmp_ref[...]), NEG_INF)
    e = jnp.exp2(sc - jnp.max(sc, axis=1, keepdims=True))
    p = e / jnp.sum(e, axis=1, keepdims=True)
    pb = jnp.where(cmask, p, 0.0).astype(BF16)
    o_cmp = jnp.dot(pb, vcmp_ref[...], preferred_element_type=F32)

    bias_rows = []
    for g in range(N_GROUPS):
        imp = None
        for i in range(HEADS_PER_GROUP):
            r0 = (g * HEADS_PER_GROUP + i) * tq
            part = _nt_dot(ovt_ref[...], pb[r0:r0 + tq, :])
            imp = part if imp is None else imp + part
        val = jnp.where(valid, imp + FORCE_BONUS * forced, NEG_INF)
        rank = jnp.zeros((nb, tq), F32)
        for k in range(nb):
            vk = val[k:k + 1, :]
            tie = jnp.where(jb > k, 1.0, 0.0)
            rank = rank + jnp.where(vk > val, 1.0, jnp.where(vk == val, tie, 0.0))
        chosen = jnp.where(valid, jnp.where(rank < N_SEL, 0.0, NEG_INF), NEG_INF)
        bias_t = jnp.concatenate([chosen, jnp.zeros((LANES - nb, tq), F32)], axis=0)
        bias_rows.extend([bias_t.T.astype(BF16)] * HEADS_PER_GROUP)
    qaug = jnp.concatenate([qst, jnp.concatenate(bias_rows, axis=0)], axis=1)

    kw = jnp.concatenate([kv_rows(_KW, win_a, 1), kv_rows(_KW, win_b, 1), kv_rows(_KW, j, 1)], axis=0)
    vw = jnp.concatenate([kv_rows(_VW, win_a, 1), kv_rows(_VW, win_b, 1), kv_rows(_VW, j, 1)], axis=0)
    sw = _nt_dot(qst, kw)
    sw = jnp.concatenate([jnp.where(win_tail, sw[:, :ck] + off_a, NEG_INF),
                          sw[:, ck:2 * ck] + off_b,
                          jnp.where(causal, sw[:, 2 * ck:], NEG_INF)], axis=1)
    pw = jnp.exp2(sw - jnp.max(sw, axis=1, keepdims=True)).astype(BF16)

    s_d = jnp.where(causal, _nt_dot(qaug, sel_keys(j, 1)), NEG_INF)
    s_ref[j] = s_d
    mx_ref[...] = jnp.maximum(s_d[:, :LANES], s_d[:, LANES:])

    def pair_score(u, carry):
        store_scores(_nt_dot(qaug, sel_keys(2 * u, 2)), 2 * u, 2)
        return carry

    lax.fori_loop(0, n_pairs, pair_score, 0)

    @pl.when(odd)
    def _():
        store_scores(_nt_dot(qaug, sel_keys(j - 1, 1)), j - 1, 1)

    m_sel = jnp.max(mx_ref[...], axis=1, keepdims=True)
    acc_ref[...] = jnp.dot(probs(j, 1, m_sel), with_ones(kv_rows(_VS, j, 1)),
                           preferred_element_type=F32)

    vw1 = with_ones(vw)
    half = rows // 2
    o_win = normalised(jnp.concatenate(
        [jnp.dot(pw[:half], vw1, preferred_element_type=F32),
         jnp.dot(pw[half:], vw1, preferred_element_type=F32)], axis=0))

    def gate(br, g, i):
        cix = br * N_HEADS + g * HEADS_PER_GROUP + i
        return gates[:, cix * LANES:(cix + 1) * LANES]

    head_rows = [((g * HEADS_PER_GROUP + i) * tq, g, i)
                 for i in range(HEADS_PER_GROUP) for g in range(N_GROUPS)]
    t_cmp = [gate(0, g, i) * o_cmp[r0:r0 + tq, :] for r0, g, i in head_rows]
    t_win = [gate(2, g, i) * o_win[r0:r0 + tq, :] for r0, g, i in head_rows]

    def pair_value(u, carry):
        acc_ref[...] = acc_ref[...] + jnp.dot(
            probs(2 * u, 2, m_sel), with_ones(kv_rows(_VS, 2 * u, 2)), preferred_element_type=F32)
        return carry

    lax.fori_loop(0, n_pairs, pair_value, 0)

    @pl.when(odd)
    def _():
        acc_ref[...] = acc_ref[...] + jnp.dot(
            probs(j - 1, 1, m_sel), with_ones(kv_rows(_VS, j - 1, 1)), preferred_element_type=F32)

    o_sel = normalised(acc_ref[...])

    tots = [(t_cmp[n] + gate(1, g, i) * o_sel[r0:r0 + tq, :]) + t_win[n]
            for n, (r0, g, i) in enumerate(head_rows)]
    for i in range(HEADS_PER_GROUP):
        o_ref[:, i * LANES:(i + 1) * LANES] = jnp.where(
            lane_q < HEAD_DIM, tots[N_GROUPS * i], tots[N_GROUPS * i + 1])


def _attn_call(q, kc, vc, kv, gt, pos_r, w1_e, w2_e, est, ovt, gx, *, bsz, seq):
    assert WINDOW == 2 * KEY_CHUNK and Q_TILE == KEY_CHUNK
    nq = seq // Q_TILE
    rows = N_HEADS * Q_TILE
    n_cmp = seq // CMP_STRIDE
    tile = lambda b, j: (b * nq + j, 0)
    per_seq = lambda b, j: (b, 0, 0)
    c2 = lambda b, j: (0, 0)
    c3 = lambda b, j: (0, 0, 0)
    c4 = lambda b, j: (0, 0, 0, 0)
    return pl.pallas_call(
        functools.partial(_attn_kernel, seq=seq),
        grid=(bsz, nq),
        in_specs=[
            pl.BlockSpec((Q_TILE, ATT_WIDTH), tile),
            pl.BlockSpec((None, seq, KV_COLS), per_seq),
            pl.BlockSpec((None, seq, KV_COLS), per_seq),
            pl.BlockSpec((None, seq, 4 * KV_COLS), per_seq),
            pl.BlockSpec((Q_TILE, LANES), tile),
            pl.BlockSpec(pos_r.shape, c3),
            pl.BlockSpec(w1_e.shape, c4),
            pl.BlockSpec(w2_e.shape, c3),
            pl.BlockSpec(est.shape, c2),
            pl.BlockSpec(ovt.shape, c2),
            pl.BlockSpec(gx.shape, c2),
        ],
        out_specs=pl.BlockSpec((Q_TILE, ATT_WIDTH), tile),
        out_shape=jax.ShapeDtypeStruct((bsz * seq, ATT_WIDTH), F32),
        scratch_shapes=[
            pltpu.VMEM((n_cmp, LANES), BF16),
            pltpu.VMEM((n_cmp, LANES), BF16),
            pltpu.VMEM((seq // KEY_CHUNK, rows, KEY_CHUNK), F32),
            pltpu.VMEM((rows, LANES), F32),
            pltpu.VMEM((rows, 2 * LANES), F32),
        ],
        compiler_params=pltpu.CompilerParams(
            dimension_semantics=("arbitrary", "arbitrary"), vmem_limit_bytes=V7X_VMEM_LIMIT),
        name="nsa_attention",
    )(q, kc, vc, kv, gt, pos_r, w1_e, w2_e, est, ovt, gx)


def _post_kernel(oa_ref, cv_ref, halo_ref, x_ref, mod_ref, cw_ref, cb_ref, go_ref, w_ref,
                 g_ref, win_ref, wout_ref, gf_ref, o_ref, *, tiles_per_seq, final):
    i = pl.program_id(0)
    cw = ATT_WIDTH
    hc = cv_ref[:, 0:cw]
    bg = cv_ref[:, cw:2 * cw]
    cg = cv_ref[:, 2 * cw:3 * cw]
    u = cg * hc
    hu = halo_ref[:, 2 * cw:3 * cw] * halo_ref[:, 0:cw]
    hu = jnp.where(i % tiles_per_seq == 0, 0.0, hu)
    row = lax.broadcasted_iota(jnp.int32, u.shape, 0)
    u1 = jnp.where(row == 0, hu[7:8, :], pltpu.roll(u, 1, 0))
    u2 = jnp.where(row == 0, hu[6:7, :], jnp.where(row == 1, hu[7:8, :], pltpu.roll(u, 2, 0)))
    v = cb_ref[...] + u2 * cw_ref[0:1, :]
    v = v + u1 * cw_ref[1:2, :]
    v = v + u * cw_ref[2:3, :]
    oc = bg * v
    oa = _rms(oa_ref[...], go_ref[:, 0:cw])
    oc = _rms(oc, go_ref[:, cw:2 * cw])
    ob = jnp.concatenate([oa, oc], axis=1).astype(BF16)
    y = jnp.dot(ob, w_ref[...], preferred_element_type=F32)
    x2 = x_ref[...] + mod_ref[5:6, :] * y
    out = _ffn(x2, mod_ref, g_ref[2:3, :], win_ref, wout_ref, 2)
    if final:
        out = _rms(out, gf_ref[...])
    o_ref[...] = out


def _post_call(oa, cv, x, mod, conv_w, conv_b, g_out_p, w_out_p, g, w_in, w_out, g_final, *, seq, final):
    t, d = x.shape
    d_ff = w_out.shape[0]
    tiles_per_seq = seq // TOKEN_TILE
    halo_blocks = TOKEN_TILE // 8
    const = lambda i: (0, 0)
    row = lambda i: (i, 0)
    return pl.pallas_call(
        functools.partial(_post_kernel, tiles_per_seq=tiles_per_seq, final=final),
        grid=(t // TOKEN_TILE,),
        in_specs=[
            pl.BlockSpec((TOKEN_TILE, oa.shape[1]), row),
            pl.BlockSpec((TOKEN_TILE, cv.shape[1]), row),
            pl.BlockSpec((8, cv.shape[1]), lambda i: (jnp.maximum(i * halo_blocks - 1, 0), 0)),
            pl.BlockSpec((TOKEN_TILE, d), row),
            pl.BlockSpec((None, 3 * N_SUB, d), lambda i: (i // tiles_per_seq, 0, 0)),
            pl.BlockSpec(conv_w.shape, const),
            pl.BlockSpec(conv_b.shape, const),
            pl.BlockSpec(g_out_p.shape, const),
            pl.BlockSpec(w_out_p.shape, const, pipeline_mode=pl.Buffered(1)),
            pl.BlockSpec((N_SUB, d), const),
            pl.BlockSpec((d, 2 * d_ff), const, pipeline_mode=pl.Buffered(1)),
            pl.BlockSpec((d_ff, d), const, pipeline_mode=pl.Buffered(1)),
            pl.BlockSpec((1, d), const),
        ],
        out_specs=pl.BlockSpec((TOKEN_TILE, d), row),
        out_shape=jax.ShapeDtypeStruct((t, d), F32),
        compiler_params=pltpu.CompilerParams(
            dimension_semantics=("arbitrary",), vmem_limit_bytes=V7X_VMEM_LIMIT),
        name="outproj_ffn_final" if final else "outproj_ffn",
    )(oa, cv, cv, x, mod, conv_w, conv_b, g_out_p, w_out_p, g, w_in, w_out, g_final)


def _head_perm():
    idx = np.empty(ATT_WIDTH, np.int32)
    for i in range(HEADS_PER_GROUP):
        for g in range(N_GROUPS):
            h = g * HEADS_PER_GROUP + i
            dst = i * LANES + g * HEAD_DIM
            idx[dst:dst + HEAD_DIM] = np.arange(h * HEAD_DIM, (h + 1) * HEAD_DIM)
    return idx


def _gate_perm():
    return np.array([h * 3 + br for br in range(3) for h in range(N_HEADS)], np.int32)


def _prep_mixer_in(w):
    offs = np.cumsum([0, ATT_WIDTH] + [KV_COLS] * 6 + [3 * N_HEADS] + [ATT_WIDTH] * 3)
    q = w[..., offs[0]:offs[1]][..., _head_perm()] * (LOG2_E * HEAD_DIM ** -0.5)
    kvs = w[..., offs[1]:offs[7]]
    gates = w[..., offs[7]:offs[8]][..., _gate_perm()]
    conv = w[..., offs[8]:offs[11]]
    pad = jnp.zeros(w.shape[:-1] + (LANES - 3 * N_HEADS,), w.dtype)
    return jnp.concatenate([q, kvs, conv, gates, pad], axis=-1).astype(BF16)


def _prep_compress(cmp_pos, w_cmp1, w_cmp2):
    depth = cmp_pos.shape[0]
    eye = jnp.eye(N_GROUPS, dtype=F32)
    pos = cmp_pos.reshape(depth, 2, CMP_BLOCK // CMP_STRIDE, CMP_STRIDE, 1, HEAD_DIM)
    pos_r = jnp.broadcast_to(pos, pos.shape[:4] + (N_GROUPS, HEAD_DIM)).reshape(
        depth, 2, CMP_BLOCK // CMP_STRIDE, CMP_STRIDE * LANES)
    w1 = w_cmp1.reshape(depth, 2, CMP_BLOCK // CMP_STRIDE, CMP_STRIDE, HEAD_DIM, HEAD_DIM)
    w1_e = jnp.einsum("lkpjde,gh->lkpjgdhe", w1, eye).reshape(
        depth, 2, CMP_BLOCK // CMP_STRIDE, CMP_STRIDE * LANES, LANES).astype(BF16)
    w2_e = jnp.einsum("lkde,gh->lkgdhe", w_cmp2, eye).reshape(depth, 2, LANES, LANES).astype(BF16)
    return pos_r, w1_e, w2_e


def _selection_constants(seq):
    n_cmp = seq // CMP_STRIDE
    nb = seq // SEL_BLOCK
    est = (np.arange(seq)[:, None] // SEL_BLOCK == np.arange(LANES)[None, :]).astype(np.float32)
    cs = np.arange(n_cmp) * CMP_STRIDE
    js = np.arange(nb) * SEL_BLOCK
    ovt = ((cs[None, :] < js[:, None] + SEL_BLOCK) & (cs[None, :] + CMP_BLOCK > js[:, None])).astype(np.float32)
    n_gates = 3 * N_HEADS
    gx_half = (np.arange(LANES)[:, None] == np.arange(n_gates * LANES)[None, :] // LANES).astype(np.float32)
    gx = np.concatenate([gx_half, gx_half], axis=0)
    return jnp.asarray(est, BF16), jnp.asarray(ovt, BF16), jnp.asarray(gx, BF16)


def kernel(x, c, w_ada, b_ada, g_norm, w_ff_in, w_ff_out, w_mix_in, cmp_pos, w_cmp1, w_cmp2,
           conv_w, conv_b, g_mix_out, w_mix_out, g_final):
    bsz, seq, d = x.shape
    depth = w_ada.shape[0]
    assert seq % TOKEN_TILE == 0 and seq % Q_TILE == 0 and seq // CMP_STRIDE == LANES
    assert w_mix_out.shape[1] == 2 * ATT_WIDTH == d

    perm = _head_perm()
    w_ff_in_b = w_ff_in.astype(BF16)
    w_ff_out_b = w_ff_out.astype(BF16)
    w_in_p = _prep_mixer_in(w_mix_in)
    row_perm = np.concatenate([perm, np.arange(ATT_WIDTH, d)])
    w_out_p = w_mix_out[:, row_perm, :].astype(BF16)
    g_out_p = g_mix_out[:, row_perm]
    pos_r, w1_e, w2_e = _prep_compress(cmp_pos, w_cmp1, w_cmp2)
    est, ovt, gx = _selection_constants(seq)

    mod_all = _ada_call(c, w_ada, b_ada).reshape(depth, bsz, 3 * N_SUB, d)
    gf = g_final.reshape(1, d)

    xt = x.reshape(bsz * seq, d)
    for l in range(depth):
        mod = mod_all[l]
        xt, q, kc, vc, kv, cv, gt = _pre_call(xt, mod, g_norm[l], w_ff_in_b[l, 0], w_ff_out_b[l, 0],
                                              w_in_p[l], seq=seq)
        oa = _attn_call(q, kc.reshape(bsz, seq, KV_COLS), vc.reshape(bsz, seq, KV_COLS),
                        kv.reshape(bsz, seq, 4 * KV_COLS), gt, pos_r[l], w1_e[l], w2_e[l], est, ovt, gx,
                        bsz=bsz, seq=seq)
        xt = _post_call(oa, cv, xt, mod, conv_w[l], conv_b[l:l + 1], g_out_p[l:l + 1], w_out_p[l],
                        g_norm[l], w_ff_in_b[l, 1], w_ff_out_b[l, 1], gf,
                        seq=seq, final=(l == depth - 1))
    return xt.reshape(bsz, seq, d)
```

```python
import functools
import math

import numpy as np
import jax
import jax.numpy as jnp
from jax import lax
from jax.experimental import pallas as pl
from jax.experimental.pallas import tpu as pltpu

F32 = jnp.float32
BF16 = jnp.bfloat16

HEAD_DIM = 64
N_HEADS = 8
N_GROUPS = 2
HEADS_PER_GROUP = N_HEADS // N_GROUPS
ATT_WIDTH = N_HEADS * HEAD_DIM
KV_COLS = N_GROUPS * HEAD_DIM
CONV_K = 3
CMP_BLOCK = 32
CMP_STRIDE = 16
SEL_BLOCK = 64
N_SEL = 8
WINDOW = 512
N_SUB = 3
EPS = 1e-6
NEG_INF = -1e30
FORCE_BONUS = 1e4
LOG2_E = math.log2(math.e)

LANES = 128
V7X_VMEM_LIMIT = 56 * 1024 * 1024

TOKEN_TILE = 512
Q_TILE = 256
KEY_CHUNK = 256
FF_CHUNKS = ((0, 1536), (1536, 2816))


def _silu(v):
    return v * jax.nn.sigmoid(v)


def _rms(v, g):
    ms = jnp.mean(v * v, axis=-1, keepdims=True)
    return (v * lax.rsqrt(ms + EPS)) * g


def _mod_rows(mod_ref, tiles_per_seq):
    b = pl.program_id(0) // tiles_per_seq
    d = mod_ref.shape[1] // (3 * N_SUB)
    return lambda r: mod_ref[pl.ds(b, 1), r * d:(r + 1) * d]


def _prenorm(x, g, mod, sub):
    return _rms(x, g) * (1.0 + mod(3 * sub + 1)) + mod(3 * sub)


def _nt_dot(a, b):
    return lax.dot_general(a, b, (((1,), (1,)), ((), ())), preferred_element_type=F32)


def _ada_kernel(c_ref, w_ref, b_ref, o_ref):
    ca = _silu(c_ref[...]).astype(BF16)
    o_ref[...] = jnp.dot(ca, w_ref[...].astype(BF16), preferred_element_type=F32) + b_ref[...]


def _ada_call(c, w_ada, b_ada):
    depth, d, n = w_ada.shape
    bsz = c.shape[0]
    tn = n // 4
    return pl.pallas_call(
        _ada_kernel,
        grid=(depth, n // tn),
        in_specs=[
            pl.BlockSpec((bsz, d), lambda l, j: (0, 0)),
            pl.BlockSpec((None, d, tn), lambda l, j: (l, 0, j)),
            pl.BlockSpec((None, 1, tn), lambda l, j: (l, 0, j)),
        ],
        out_specs=pl.BlockSpec((None, bsz, tn), lambda l, j: (l, 0, j)),
        out_shape=jax.ShapeDtypeStruct((depth, bsz, n), F32),
        compiler_params=pltpu.CompilerParams(
            dimension_semantics=("arbitrary", "arbitrary"), vmem_limit_bytes=V7X_VMEM_LIMIT),
        name="adaln_mod",
    )(c, w_ada, b_ada.reshape(depth, 1, n))


def _ffn(x, mod, g, win_ref, wout_ref, sub):
    d_ff = wout_ref.shape[0]
    hb = _prenorm(x, g, mod, sub).astype(BF16)
    y = None
    for c0, c1 in FF_CHUNKS:
        gate = jnp.dot(hb, win_ref[:, c0:c1], preferred_element_type=F32)
        up = jnp.dot(hb, win_ref[:, d_ff + c0:d_ff + c1], preferred_element_type=F32)
        act = (_silu(gate) * up).astype(BF16)
        part = jnp.dot(act, wout_ref[c0:c1, :], preferred_element_type=F32)
        y = part if y is None else y + part
    return x + (0.5 * mod(3 * sub + 2)) * y


_Q0, _KC0, _VC0, _KV0, _CV0, _GT0, _IN_P = 0, 512, 640, 768, 1280, 2816, 2944
_KS, _VS, _KW, _VW = 0, LANES, 2 * LANES, 3 * LANES


def _pre_kernel(x_ref, mod_ref, g_ref, win_ref, wout_ref, wmix_ref,
                x1_ref, q_ref, kc_ref, vc_ref, kv_ref, cv_ref, gt_ref, *, tiles_per_seq):
    mod = _mod_rows(mod_ref, tiles_per_seq)
    x1 = _ffn(x_ref[...], mod, g_ref[0:1, :], win_ref, wout_ref, 0)
    x1_ref[...] = x1
    hb = _prenorm(x1, g_ref[1:2, :], mod, 1).astype(BF16)

    def proj(c0, c1):
        return jnp.dot(hb, wmix_ref[:, c0:c1], preferred_element_type=F32)

    q_ref[...] = proj(_Q0, _KC0).astype(BF16)
    kcv = proj(_KC0, _KV0)
    kc_ref[...] = kcv[:, :KV_COLS]
    vc_ref[...] = kcv[:, KV_COLS:]
    kv_ref[...] = proj(_KV0, _CV0).astype(BF16)
    cv_ref[...] = proj(_CV0, _GT0)
    gt_ref[...] = proj(_GT0, _IN_P)


def _layer_spec(arr, l, *fixed, single_buffer=False):
    lead = (l,) + fixed
    rest = arr.shape[len(lead):]
    idx = lead + (0,) * len(rest)
    kwargs = {"pipeline_mode": pl.Buffered(1)} if single_buffer else {}
    return pl.BlockSpec((None,) * len(lead) + rest, lambda *_: idx, **kwargs)


def _pre_call(x, mod_all, g_norm, w_ff_in, w_ff_out, w_mix_p, l, *, seq):
    t, d = x.shape
    tiles_per_seq = seq // TOKEN_TILE
    row = lambda i: (i, 0)
    widths = (d, _KC0 - _Q0, KV_COLS, KV_COLS, _CV0 - _KV0, _GT0 - _CV0, _IN_P - _GT0)
    dtypes = (F32, BF16, F32, F32, BF16, F32, F32)
    return pl.pallas_call(
        functools.partial(_pre_kernel, tiles_per_seq=tiles_per_seq),
        grid=(t // TOKEN_TILE,),
        in_specs=[
            pl.BlockSpec((TOKEN_TILE, d), row),
            _layer_spec(mod_all, l),
            _layer_spec(g_norm, l),
            _layer_spec(w_ff_in, l, 0, single_buffer=True),
            _layer_spec(w_ff_out, l, 0, single_buffer=True),
            _layer_spec(w_mix_p, l, single_buffer=True),
        ],
        out_specs=[pl.BlockSpec((TOKEN_TILE, w), row) for w in widths],
        out_shape=[jax.ShapeDtypeStruct((t, w), dt) for w, dt in zip(widths, dtypes)],
        compiler_params=pltpu.CompilerParams(
            dimension_semantics=("arbitrary",), vmem_limit_bytes=V7X_VMEM_LIMIT),
        name="ffn_inproj",
    )(x, mod_all, g_norm, w_ff_in, w_ff_out, w_mix_p)


def _attn_kernel(q_ref, kc_ref, vc_ref, kv_ref, gt_ref, pos_ref, w1_ref, w2_ref, est_ref, ovt_ref, gx_ref,
                 o_ref, kcmp_ref, vcmp_ref, s_ref, mx_ref, acc_ref, *, seq):
    j = pl.program_id(1)
    tq = Q_TILE
    ck = KEY_CHUNK
    rows = N_HEADS * tq
    n_cmp = seq // CMP_STRIDE
    per_row = CMP_BLOCK // CMP_STRIDE
    nb = seq // SEL_BLOCK

    @pl.when(j == 0)
    def _():
        for kvi, (src, dst) in enumerate(((kc_ref, kcmp_ref), (vc_ref, vcmp_ref))):
            r = jnp.concatenate(
                [src[pl.ds(jj, n_cmp, stride=CMP_STRIDE), :] for jj in range(CMP_STRIDE)], axis=1)
            pre = None
            for part in range(per_row):
                xin = (r + pos_ref[kvi, part:part + 1, :]).astype(BF16)
                prod = jnp.dot(xin, w1_ref[kvi, part], preferred_element_type=F32)
                if part:
                    prod = pltpu.roll(prod, n_cmp - part, 0)
                pre = prod if pre is None else pre + prod
            act = _silu(pre).astype(BF16)
            dst[...] = jnp.dot(act, w2_ref[kvi], preferred_element_type=F32).astype(BF16)

    t0 = j * tq
    lane_q = lax.broadcasted_iota(jnp.int32, (tq, LANES), 1)
    q_loc = lax.broadcasted_iota(jnp.int32, (rows, ck), 0) & (tq - 1)
    key_loc = lax.broadcasted_iota(jnp.int32, (rows, ck), 1)
    causal = key_loc <= q_loc
    win_tail = key_loc > q_loc
    q_loc_c = lax.broadcasted_iota(jnp.int32, (rows, n_cmp), 0) & (tq - 1)
    cmask = (lax.broadcasted_iota(jnp.int32, (rows, n_cmp), 1) * CMP_STRIDE + (CMP_BLOCK - 1)) <= t0 + q_loc_c
    sg = jax.nn.sigmoid(gt_ref[...])
    sg_hi = sg.astype(BF16)
    sg_lo = (sg - sg_hi.astype(F32)).astype(BF16)
    gates = jnp.dot(jnp.concatenate([sg_hi, sg_lo], axis=1), gx_ref[...],
                    preferred_element_type=F32)

    jb = lax.broadcasted_iota(jnp.int32, (nb, tq), 0)
    cur = (t0 + lax.broadcasted_iota(jnp.int32, (nb, tq), 1)) // SEL_BLOCK
    valid = jb <= cur
    forced = ((jb == 0) | (jb == cur) | (jb == cur - 1)).astype(F32)

    def kv_rows(col, c0, n):
        return kv_ref[pl.ds(pl.multiple_of(c0 * ck, ck), n * ck), col:col + LANES]

    def sel_keys(c0, n):
        est = est_ref[pl.ds(pl.multiple_of(c0 * ck, ck), n * ck), :]
        return jnp.concatenate([kv_rows(_KS, c0, n), est], axis=1)

    def with_ones(v):
        return jnp.concatenate([v, jnp.ones(v.shape, BF16)], axis=1)

    def store_scores(s, c0, n):
        smax = None
        for k in range(n):
            s_ref[c0 + k] = s[:, k * ck:(k + 1) * ck]
            for h in range(ck // LANES):
                part = s[:, k * ck + h * LANES:k * ck + (h + 1) * LANES]
                smax = part if smax is None else jnp.maximum(smax, part)
        mx_ref[...] = jnp.maximum(mx_ref[...], smax)

    def probs(c0, n, m):
        s = s_ref[c0] if n == 1 else jnp.concatenate([s_ref[c0 + k] for k in range(n)], axis=1)
        return jnp.exp2(s - m).astype(BF16)

    def normalised(ov):
        return ov[:, :LANES] / ov[:, LANES:]

    n_pairs = lax.shift_right_logical(j, 1)
    odd = (j & 1) == 1
    win_a = jnp.maximum(j - 2, 0)
    win_b = jnp.maximum(j - 1, 0)
    off_a = jnp.where(j >= 2, 0.0, NEG_INF)
    off_b = jnp.where(j >= 1, 0.0, NEG_INF)

    qs = []
    for g in range(N_GROUPS):
        keep = (lane_q < HEAD_DIM) if g == 0 else (lane_q >= HEAD_DIM)
        for i in range(HEADS_PER_GROUP):
            qi = q_ref[:, i * LANES:(i + 1) * LANES]
            qs.append(jnp.where(keep, qi, jnp.zeros_like(qi)))
    qst = jnp.concatenate(qs, axis=0)

    sc = jnp.where(cmask, _nt_dot(qst, kcmp_ref[...]), NEG_INF)
    e = jnp.exp2(sc - jnp.max(sc, axis=1, keepdims=True))
    p = e / jnp.sum(e, axis=1, keepdims=True)
    pb = jnp.where(cmask, p, 0.0).astype(BF16)
    o_cmp = jnp.dot(pb, vcmp_ref[...], preferred_element_type=F32)

    bias_rows = []
    for g in range(N_GROUPS):
        imp = None
        for i in range(HEADS_PER_GROUP):
            r0 = (g * HEADS_PER_GROUP + i) * tq
            part = _nt_dot(ovt_ref[...], pb[r0:r0 + tq, :])
            imp = part if imp is None else imp + part
        val = jnp.where(valid, imp + FORCE_BONUS * forced, NEG_INF)
        rank = jnp.zeros((nb, tq), F32)
        for k in range(nb):
            vk = val[k:k + 1, :]
            tie = jnp.where(jb > k, 1.0, 0.0)
            rank = rank + jnp.where(vk > val, 1.0, jnp.where(vk == val, tie, 0.0))
        chosen = jnp.where(valid, jnp.where(rank < N_SEL, 0.0, NEG_INF), NEG_INF)
        bias_t = jnp.concatenate([chosen, jnp.zeros((LANES - nb, tq), F32)], axis=0)
        bias_rows.extend([bias_t.T.astype(BF16)] * HEADS_PER_GROUP)
    qaug = jnp.concatenate([qst, jnp.concatenate(bias_rows, axis=0)], axis=1)

    kw = jnp.concatenate([kv_rows(_KW, win_a, 1), kv_rows(_KW, win_b, 1), kv_rows(_KW, j, 1)], axis=0)
    vw = jnp.concatenate([kv_rows(_VW, win_a, 1), kv_rows(_VW, win_b, 1), kv_rows(_VW, j, 1)], axis=0)
    sw = _nt_dot(qst, kw)
    sw = jnp.concatenate([jnp.where(win_tail, sw[:, :ck] + off_a, NEG_INF),
                          sw[:, ck:2 * ck] + off_b,
                          jnp.where(causal, sw[:, 2 * ck:], NEG_INF)], axis=1)
    pw = jnp.exp2(sw - jnp.max(sw, axis=1, keepdims=True)).astype(BF16)

    s_d = jnp.where(causal, _nt_dot(qaug, sel_keys(j, 1)), NEG_INF)
    s_ref[j] = s_d
    mx_ref[...] = jnp.maximum(s_d[:, :LANES], s_d[:, LANES:])

    def pair_score(u, carry):
        store_scores(_nt_dot(qaug, sel_keys(2 * u, 2)), 2 * u, 2)
        return carry

    lax.fori_loop(0, n_pairs, pair_score, 0)

    @pl.when(odd)
    def _():
        store_scores(_nt_dot(qaug, sel_keys(j - 1, 1)), j - 1, 1)

    m_sel = jnp.max(mx_ref[...], axis=1, keepdims=True)
    acc_ref[...] = jnp.dot(probs(j, 1, m_sel), with_ones(kv_rows(_VS, j, 1)),
                           preferred_element_type=F32)

    vw1 = with_ones(vw)
    half = rows // 2
    o_win = normalised(jnp.concatenate(
        [jnp.dot(pw[:half], vw1, preferred_element_type=F32),
         jnp.dot(pw[half:], vw1, preferred_element_type=F32)], axis=0))

    def gate(br, g, i):
        cix = br * N_HEADS + g * HEADS_PER_GROUP + i
        return gates[:, cix * LANES:(cix + 1) * LANES]

    head_rows = [((g * HEADS_PER_GROUP + i) * tq, g, i)
                 for i in range(HEADS_PER_GROUP) for g in range(N_GROUPS)]
    t_cmp = [gate(0, g, i) * o_cmp[r0:r0 + tq, :] for r0, g, i in head_rows]
    t_win = [gate(2, g, i) * o_win[r0:r0 + tq, :] for r0, g, i in head_rows]

    def pair_value(u, carry):
        acc_ref[...] = acc_ref[...] + jnp.dot(
            probs(2 * u, 2, m_sel), with_ones(kv_rows(_VS, 2 * u, 2)), preferred_element_type=F32)
        return carry

    lax.fori_loop(0, n_pairs, pair_value, 0)

    @pl.when(odd)
    def _():
        acc_ref[...] = acc_ref[...] + jnp.dot(
            probs(j - 1, 1, m_sel), with_ones(kv_rows(_VS, j - 1, 1)), preferred_element_type=F32)

    o_sel = normalised(acc_ref[...])

    tots = [(t_cmp[n] + gate(1, g, i) * o_sel[r0:r0 + tq, :]) + t_win[n]
            for n, (r0, g, i) in enumerate(head_rows)]
    for i in range(HEADS_PER_GROUP):
        o_ref[:, i * LANES:(i + 1) * LANES] = jnp.where(
            lane_q < HEAD_DIM, tots[N_GROUPS * i], tots[N_GROUPS * i + 1])


def _attn_call(q, kc, vc, kv, gt, pos_r, w1_e, w2_e, est, ovt, gx, l, *, bsz, seq):
    assert WINDOW == 2 * KEY_CHUNK and Q_TILE == KEY_CHUNK
    nq = seq // Q_TILE
    rows = N_HEADS * Q_TILE
    n_cmp = seq // CMP_STRIDE
    tile = lambda b, j: (b * nq + j, 0)
    per_seq = lambda b, j: (b, 0, 0)
    c2 = lambda b, j: (0, 0)
    return pl.pallas_call(
        functools.partial(_attn_kernel, seq=seq),
        grid=(bsz, nq),
        in_specs=[
            pl.BlockSpec((Q_TILE, ATT_WIDTH), tile),
            pl.BlockSpec((None, seq, KV_COLS), per_seq),
            pl.BlockSpec((None, seq, KV_COLS), per_seq),
            pl.BlockSpec((None, seq, 4 * KV_COLS), per_seq),
            pl.BlockSpec((Q_TILE, LANES), tile),
            _layer_spec(pos_r, l),
            _layer_spec(w1_e, l, single_buffer=True),
            _layer_spec(w2_e, l),
            pl.BlockSpec(est.shape, c2, pipeline_mode=pl.Buffered(1)),
            pl.BlockSpec(ovt.shape, c2),
            pl.BlockSpec(gx.shape, c2, pipeline_mode=pl.Buffered(1)),
        ],
        out_specs=pl.BlockSpec((Q_TILE, ATT_WIDTH), tile),
        out_shape=jax.ShapeDtypeStruct((bsz * seq, ATT_WIDTH), F32),
        scratch_shapes=[
            pltpu.VMEM((n_cmp, LANES), BF16),
            pltpu.VMEM((n_cmp, LANES), BF16),
            pltpu.VMEM((seq // KEY_CHUNK, rows, KEY_CHUNK), F32),
            pltpu.VMEM((rows, LANES), F32),
            pltpu.VMEM((rows, 2 * LANES), F32),
        ],
        compiler_params=pltpu.CompilerParams(
            dimension_semantics=("arbitrary", "arbitrary"), vmem_limit_bytes=V7X_VMEM_LIMIT),
        name="nsa_attention",
    )(q, kc, vc, kv, gt, pos_r, w1_e, w2_e, est, ovt, gx)


def _post_kernel(oa_ref, cv_ref, halo_ref, x_ref, mod_ref, cw_ref, cb_ref, go_ref, w_ref,
                 g_ref, win_ref, wout_ref, gf_ref, o_ref, *, tiles_per_seq, final):
    i = pl.program_id(0)
    cw = ATT_WIDTH
    hc = cv_ref[:, 0:cw]
    bg = cv_ref[:, cw:2 * cw]
    cg = cv_ref[:, 2 * cw:3 * cw]
    u = cg * hc
    hu = halo_ref[:, 2 * cw:3 * cw] * halo_ref[:, 0:cw]
    hu = jnp.where(i % tiles_per_seq == 0, 0.0, hu)
    row = lax.broadcasted_iota(jnp.int32, u.shape, 0)
    u1 = jnp.where(row == 0, hu[7:8, :], pltpu.roll(u, 1, 0))
    u2 = jnp.where(row == 0, hu[6:7, :], jnp.where(row == 1, hu[7:8, :], pltpu.roll(u, 2, 0)))
    v = cb_ref[...] + u2 * cw_ref[0:1, :]
    v = v + u1 * cw_ref[1:2, :]
    v = v + u * cw_ref[2:3, :]
    oc = bg * v
    oa = _rms(oa_ref[...], go_ref[:, 0:cw])
    oc = _rms(oc, go_ref[:, cw:2 * cw])
    ob = jnp.concatenate([oa, oc], axis=1).astype(BF16)
    y = jnp.dot(ob, w_ref[...], preferred_element_type=F32)
    mod = _mod_rows(mod_ref, tiles_per_seq)
    x2 = x_ref[...] + mod(5) * y
    out = _ffn(x2, mod, g_ref[2:3, :], win_ref, wout_ref, 2)
    if final:
        out = _rms(out, gf_ref[...])
    o_ref[...] = out


def _post_call(oa, cv, x, mod_all, conv_w, conv_b, g_out_p, w_out_p, g_norm, w_ff_in, w_ff_out, g_final,
               l, *, seq, final):
    t, d = x.shape
    tiles_per_seq = seq // TOKEN_TILE
    halo_blocks = TOKEN_TILE // 8
    const = lambda i: (0, 0)
    row = lambda i: (i, 0)
    return pl.pallas_call(
        functools.partial(_post_kernel, tiles_per_seq=tiles_per_seq, final=final),
        grid=(t // TOKEN_TILE,),
        in_specs=[
            pl.BlockSpec((TOKEN_TILE, oa.shape[1]), row),
            pl.BlockSpec((TOKEN_TILE, cv.shape[1]), row),
            pl.BlockSpec((8, cv.shape[1]), lambda i: (jnp.maximum(i * halo_blocks - 1, 0), 0)),
            pl.BlockSpec((TOKEN_TILE, d), row),
            _layer_spec(mod_all, l),
            _layer_spec(conv_w, l),
            _layer_spec(conv_b, l),
            _layer_spec(g_out_p, l),
            _layer_spec(w_out_p, l, single_buffer=True),
            _layer_spec(g_norm, l),
            _layer_spec(w_ff_in, l, 1, single_buffer=True),
            _layer_spec(w_ff_out, l, 1, single_buffer=True),
            pl.BlockSpec((1, d), const),
        ],
        out_specs=pl.BlockSpec((TOKEN_TILE, d), row),
        out_shape=jax.ShapeDtypeStruct((t, d), F32),
        compiler_params=pltpu.CompilerParams(
            dimension_semantics=("arbitrary",), vmem_limit_bytes=V7X_VMEM_LIMIT),
        name="outproj_ffn_final" if final else "outproj_ffn",
    )(oa, cv, cv, x, mod_all, conv_w, conv_b, g_out_p, w_out_p, g_norm, w_ff_in, w_ff_out, g_final)


def _head_perm():
    idx = np.empty(ATT_WIDTH, np.int32)
    for i in range(HEADS_PER_GROUP):
        for g in range(N_GROUPS):
            h = g * HEADS_PER_GROUP + i
            dst = i * LANES + g * HEAD_DIM
            idx[dst:dst + HEAD_DIM] = np.arange(h * HEAD_DIM, (h + 1) * HEAD_DIM)
    return idx


def _gate_perm():
    return np.array([h * 3 + br for br in range(3) for h in range(N_HEADS)], np.int32)


def _prep_mixer_in(w):
    offs = np.cumsum([0, ATT_WIDTH] + [KV_COLS] * 6 + [3 * N_HEADS] + [ATT_WIDTH] * 3)
    q = w[..., offs[0]:offs[1]][..., _head_perm()] * (LOG2_E * HEAD_DIM ** -0.5)
    kvs = w[..., offs[1]:offs[7]]
    gates = w[..., offs[7]:offs[8]][..., _gate_perm()]
    conv = w[..., offs[8]:offs[11]]
    pad = jnp.zeros(w.shape[:-1] + (LANES - 3 * N_HEADS,), w.dtype)
    return jnp.concatenate([q, kvs, conv, gates, pad], axis=-1).astype(BF16)


def _prep_compress(cmp_pos, w_cmp1, w_cmp2):
    depth = cmp_pos.shape[0]
    eye = jnp.eye(N_GROUPS, dtype=F32)
    pos = cmp_pos.reshape(depth, 2, CMP_BLOCK // CMP_STRIDE, CMP_STRIDE, 1, HEAD_DIM)
    pos_r = jnp.broadcast_to(pos, pos.shape[:4] + (N_GROUPS, HEAD_DIM)).reshape(
        depth, 2, CMP_BLOCK // CMP_STRIDE, CMP_STRIDE * LANES)
    w1 = w_cmp1.reshape(depth, 2, CMP_BLOCK // CMP_STRIDE, CMP_STRIDE, HEAD_DIM, HEAD_DIM)
    w1_e = jnp.einsum("lkpjde,gh->lkpjgdhe", w1, eye).reshape(
        depth, 2, CMP_BLOCK // CMP_STRIDE, CMP_STRIDE * LANES, LANES).astype(BF16)
    w2_e = jnp.einsum("lkde,gh->lkgdhe", w_cmp2, eye).reshape(depth, 2, LANES, LANES).astype(BF16)
    return pos_r, w1_e, w2_e


def _selection_constants(seq):
    n_cmp = seq // CMP_STRIDE
    nb = seq // SEL_BLOCK
    est = (np.arange(seq)[:, None] // SEL_BLOCK == np.arange(LANES)[None, :]).astype(np.float32)
    cs = np.arange(n_cmp) * CMP_STRIDE
    js = np.arange(nb) * SEL_BLOCK
    ovt = ((cs[None, :] < js[:, None] + SEL_BLOCK) & (cs[None, :] + CMP_BLOCK > js[:, None])).astype(np.float32)
    n_gates = 3 * N_HEADS
    gx_half = (np.arange(LANES)[:, None] == np.arange(n_gates * LANES)[None, :] // LANES).astype(np.float32)
    gx = np.concatenate([gx_half, gx_half], axis=0)
    return jnp.asarray(est, BF16), jnp.asarray(ovt, BF16), jnp.asarray(gx, BF16)


def kernel(x, c, w_ada, b_ada, g_norm, w_ff_in, w_ff_out, w_mix_in, cmp_pos, w_cmp1, w_cmp2,
           conv_w, conv_b, g_mix_out, w_mix_out, g_final):
    bsz, seq, d = x.shape
    depth = w_ada.shape[0]
    assert seq % TOKEN_TILE == 0 and seq % Q_TILE == 0 and seq // CMP_STRIDE == LANES
    assert w_mix_out.shape[1] == 2 * ATT_WIDTH == d

    perm = _head_perm()
    w_ff_in_b = w_ff_in.astype(BF16)
    w_ff_out_b = w_ff_out.astype(BF16)
    w_in_p = _prep_mixer_in(w_mix_in)
    row_perm = np.concatenate([perm, np.arange(ATT_WIDTH, d)])
    w_out_p = w_mix_out[:, row_perm, :].astype(BF16)
    g_out_p = g_mix_out[:, row_perm].reshape(depth, 1, d)
    conv_b3 = conv_b.reshape(depth, 1, -1)
    pos_r, w1_e, w2_e = _prep_compress(cmp_pos, w_cmp1, w_cmp2)
    est, ovt, gx = _selection_constants(seq)

    mod_all = _ada_call(c, w_ada, b_ada)
    gf = g_final.reshape(1, d)

    xt = x.reshape(bsz * seq, d)
    for l in range(depth):
        xt, q, kc, vc, kv, cv, gt = _pre_call(xt, mod_all, g_norm, w_ff_in_b, w_ff_out_b, w_in_p, l,
                                              seq=seq)
        oa = _attn_call(q, kc.reshape(bsz, seq, KV_COLS), vc.reshape(bsz, seq, KV_COLS),
                        kv.reshape(bsz, seq, 4 * KV_COLS), gt, pos_r, w1_e, w2_e, est, ovt, gx, l,
                        bsz=bsz, seq=seq)
        xt = _post_call(oa, cv, xt, mod_all, conv_w, conv_b3, g_out_p, w_out_p,
                        g_norm, w_ff_in_b, w_ff_out_b, gf, l, seq=seq, final=(l == depth - 1))
    return xt.reshape(bsz, seq, d)
```
